```python
import math
import jax, jax.numpy as jnp
from jax import lax
import numpy as np

D_MODEL = 1024
BATCH = 4
SEQ = 4096
DEPTH = 4
DEC_BATCH = 128
DEC_SEQ = 8
PAST_LEN = 8192
PAGE_SIZE = 128

N_MIXERS = 2
N_LAYERS_A = (DEPTH + N_MIXERS - 1) // N_MIXERS
N_LAYERS_B = DEPTH // N_MIXERS
HEAD_DIM = 64
ROT_DIM = HEAD_DIM // 4
ROPE_THETA = 500000.0
WINDOW = 128
A_HEADS = D_MODEL // HEAD_DIM
A_KV_HEADS = max(1, A_HEADS // 8)
A_GROUP = A_HEADS // A_KV_HEADS
B_HEADS = D_MODEL // (2 * HEAD_DIM)
B_KV_HEADS = B_HEADS // 2
B_GROUP = B_HEADS // B_KV_HEADS
B_VDIM = 2 * HEAD_DIM
Q_BLOCK = 128
D_FF = 4 * D_MODEL
RMS_EPS = 1e-5
N_MOD = 6
F32 = jnp.float32

kernel_name = 'hybrid_swa_sink_diffattn_adaln_decoder'


def rmsnorm(x, w):
    xf = x.astype(F32)
    y = xf * lax.rsqrt(jnp.mean(xf * xf, axis=-1, keepdims=True) + RMS_EPS)
    return y.astype(x.dtype) * w


def rope(x, pos):
    half = ROT_DIM // 2
    inv = ROPE_THETA ** (-jnp.arange(half, dtype=F32) / half)
    ang = pos.astype(F32)[:, None] * inv[None, :]
    cos = jnp.cos(ang)[None, :, None, :].astype(x.dtype)
    sin = jnp.sin(ang)[None, :, None, :].astype(x.dtype)
    x1 = x[..., :half]
    x2 = x[..., half:ROT_DIM]
    return jnp.concatenate([x1 * cos - x2 * sin, x2 * cos + x1 * sin, x[..., ROT_DIM:]], axis=-1)


def modulation(c, w, b):
    mods = (jax.nn.silu(c) @ w + b)[:, None, :]
    return jnp.split(mods, N_MOD, axis=-1)


def adaln(x, w, shift, scale):
    return rmsnorm(x, w) * (1.0 + scale) + shift


def sq_relu_mlp(h, w1, w2):
    return jnp.square(jax.nn.relu(h @ w1)) @ w2


def sink_softmax(s, valid, sink):
    s = jnp.where(valid, s, -jnp.inf)
    m = jnp.maximum(jnp.max(s, axis=-1, keepdims=True), sink)
    e = jnp.exp(s - m)
    return e / (jnp.sum(e, axis=-1, keepdims=True) + jnp.exp(sink - m))


def swa_qkv(h, w_qkv, pos):
    b, s, _ = h.shape
    qkv = h @ w_qkv
    nq = A_HEADS * HEAD_DIM
    nk = A_KV_HEADS * HEAD_DIM
    q = rope(qkv[..., :nq].reshape(b, s, A_HEADS, HEAD_DIM), pos)
    k = rope(qkv[..., nq:nq + nk].reshape(b, s, A_KV_HEADS, HEAD_DIM), pos)
    v = qkv[..., nq + nk:].reshape(b, s, A_KV_HEADS, HEAD_DIM)
    return q, k, v


def swa_prompt(h, w_qkv, w_o, sinks):
    b, s, _ = h.shape
    nb = s // WINDOW
    q, k, v = swa_qkv(h, w_qkv, jnp.arange(s))
    qb = q.reshape(b, nb, WINDOW, A_KV_HEADS, A_GROUP, HEAD_DIM)

    def band(t):
        tp = jnp.pad(t, ((0, 0), (WINDOW, 0), (0, 0), (0, 0))).reshape(b, nb + 1, WINDOW, A_KV_HEADS, HEAD_DIM)
        return jnp.concatenate([tp[:, :-1], tp[:, 1:]], axis=2)

    kb, vb = band(k), band(v)
    sc = jnp.einsum('bnqkgd,bnmkd->bnkgqm', qb, kb).astype(F32) * (HEAD_DIM ** -0.5)
    blk = jnp.arange(nb)[:, None] * WINDOW
    qpos = blk + jnp.arange(WINDOW)[None, :]
    kpos = blk - WINDOW + jnp.arange(2 * WINDOW)[None, :]
    rel = qpos[:, :, None] - kpos[:, None, :]
    valid = (rel >= 0) & (rel < WINDOW) & (kpos[:, None, :] >= 0)
    sink = sinks.astype(F32).reshape(A_KV_HEADS, A_GROUP, 1, 1)
    p = sink_softmax(sc, valid[None, :, None, None], sink)
    o = jnp.einsum('bnkgqm,bnmkd->bnqkgd', p.astype(v.dtype), vb).reshape(b, s, A_HEADS * HEAD_DIM)
    keep = min(WINDOW, s)
    return o @ w_o, k[:, s - keep:], v[:, s - keep:]


def swa_sample(h, buf_k, buf_v, w_qkv, w_o, sinks):
    b, t, _ = h.shape
    wb = buf_k.shape[1]
    qpos = PAST_LEN + jnp.arange(t)
    q, k, v = swa_qkv(h, w_qkv, qpos)
    k_all = jnp.concatenate([buf_k, k], axis=1)
    v_all = jnp.concatenate([buf_v, v], axis=1)
    kpos = PAST_LEN - wb + jnp.arange(wb + t)
    rel = qpos[:, None] - kpos[None, :]
    valid = (rel >= 0) & (rel < WINDOW)
    qg = q.reshape(b, t, A_KV_HEADS, A_GROUP, HEAD_DIM)
    sc = jnp.einsum('bqkgd,bmkd->bkgqm', qg, k_all).astype(F32) * (HEAD_DIM ** -0.5)
    sink = sinks.astype(F32).reshape(A_KV_HEADS, A_GROUP, 1, 1)
    p = sink_softmax(sc, valid, sink)
    o = jnp.einsum('bkgqm,bmkd->bqkgd', p.astype(v.dtype), v_all).reshape(b, t, A_HEADS * HEAD_DIM)
    return o @ w_o, k_all[:, t:], v_all[:, t:]


def lambda_init(layer):
    return 0.8 - 0.6 * math.exp(-0.3 * layer)


def diff_lambda(lam_p, layer):
    lf = lam_p.astype(F32)
    return jnp.exp(jnp.sum(lf[0] * lf[1])) - jnp.exp(jnp.sum(lf[2] * lf[3])) + lambda_init(layer)


def diff_qkv(h, w_qkv, pos):
    b, s, _ = h.shape
    qkv = h @ w_qkv
    nq = B_HEADS * 2 * HEAD_DIM
    nk = B_KV_HEADS * 2 * HEAD_DIM
    q = rope(qkv[..., :nq].reshape(b, s, 2 * B_HEADS, HEAD_DIM), pos).reshape(b, s, B_KV_HEADS, B_GROUP, 2, HEAD_DIM)
    k = rope(qkv[..., nq:nq + nk].reshape(b, s, 2 * B_KV_HEADS, HEAD_DIM), pos).reshape(b, s, B_KV_HEADS, 2, HEAD_DIM)
    v = qkv[..., nq + nk:].reshape(b, s, B_KV_HEADS, B_VDIM)
    return q, k, v


def diff_out(o, subln, w_o, layer):
    b, s = o.shape[:2]
    o = rmsnorm(o, subln) * (1.0 - lambda_init(layer))
    return o.reshape(b, s, B_HEADS * B_VDIM) @ w_o


def diff_prompt(h, w_qkv, w_o, lam_p, subln, layer):
    b, s, _ = h.shape
    nb = s // Q_BLOCK
    q, k, v = diff_qkv(h, w_qkv, jnp.arange(s))
    lam = diff_lambda(lam_p, layer)
    kpos = jnp.arange(s)
    qb = jnp.moveaxis(q.reshape(b, nb, Q_BLOCK, B_KV_HEADS, B_GROUP, 2, HEAD_DIM), 1, 0)

    def one_block(args):
        qblk, i = args
        sc = jnp.einsum('bqkgcd,bmkcd->bkgcqm', qblk, k).astype(F32) * (HEAD_DIM ** -0.5)
        qpos = i * Q_BLOCK + jnp.arange(Q_BLOCK)
        causal = qpos[:, None] >= kpos[None, :]
        p = jax.nn.softmax(jnp.where(causal, sc, -jnp.inf), axis=-1)
        a = p[:, :, :, 0] - lam * p[:, :, :, 1]
        return jnp.einsum('bkgqm,bmkd->bqkgd', a.astype(v.dtype), v)

    o = lax.map(one_block, (qb, jnp.arange(nb)))
    o = jnp.moveaxis(o, 0, 1).reshape(b, s, B_KV_HEADS, B_GROUP, B_VDIM)
    return diff_out(o, subln, w_o, layer), k.reshape(b, s, B_KV_HEADS, 2 * HEAD_DIM), v


def diff_sample(h, cache_k, cache_v, page_table, idx, w_qkv, w_o, lam_p, subln, layer):
    b, t, _ = h.shape
    q, k, v = diff_qkv(h, w_qkv, PAST_LEN + jnp.arange(t))
    lam = diff_lambda(lam_p, layer)
    k_past = cache_k[idx, page_table].reshape(b, -1, B_KV_HEADS, 2, HEAD_DIM)
    v_past = cache_v[idx, page_table].reshape(b, -1, B_KV_HEADS, B_VDIM)
    past = k_past.shape[1]
    scale = HEAD_DIM ** -0.5
    s_past = jnp.einsum('bqkgcd,bmkcd->bkgcqm', q, k_past).astype(F32) * scale
    s_new = jnp.einsum('bqkgcd,bmkcd->bkgcqm', q, k).astype(F32) * scale
    causal = jnp.arange(t)[:, None] >= jnp.arange(t)[None, :]
    s_new = jnp.where(causal, s_new, -jnp.inf)
    p = jax.nn.softmax(jnp.concatenate([s_past, s_new], axis=-1), axis=-1)
    a = p[:, :, :, 0] - lam * p[:, :, :, 1]
    o = (jnp.einsum('bkgqm,bmkd->bqkgd', a[..., :past].astype(v.dtype), v_past)
         + jnp.einsum('bkgqm,bmkd->bqkgd', a[..., past:].astype(v.dtype), v))
    return diff_out(o, subln, w_o, layer), k.reshape(b, t, B_KV_HEADS, 2 * HEAD_DIM), v


def setup_inputs(seed: int = 0) -> dict:
    key = jax.random.key(seed)
    ks = jax.random.split(key, 24)

    def nrm(k, shape, scale):
        return jax.random.normal(k, shape, F32) * scale

    n_pages = PAST_LEN // PAGE_SIZE
    n_used = DEC_BATCH * n_pages
    n_pool = n_used + max(1, n_used // 4)
    win_buf = min(WINDOW, PAST_LEN)
    a_qkv_w = (A_HEADS + 2 * A_KV_HEADS) * HEAD_DIM
    b_qkv_w = B_HEADS * 2 * HEAD_DIM + B_KV_HEADS * 2 * HEAD_DIM + B_KV_HEADS * B_VDIM
    page_table = jax.random.permutation(ks[8], n_pool)[:n_used].reshape(DEC_BATCH, n_pages).astype(jnp.int32)
    return {
        'x_prompt': nrm(ks[0], (BATCH, SEQ, D_MODEL), 1.0),
        'x_sample': nrm(ks[1], (DEC_BATCH, DEC_SEQ, D_MODEL), 1.0),
        'c_prompt': nrm(ks[2], (BATCH, D_MODEL), 1.0),
        'c_sample': nrm(ks[3], (DEC_BATCH, D_MODEL), 1.0),
        'state_win_k': nrm(ks[4], (N_LAYERS_A, DEC_BATCH, win_buf, A_KV_HEADS, HEAD_DIM), 1.0),
        'state_win_v': nrm(ks[5], (N_LAYERS_A, DEC_BATCH, win_buf, A_KV_HEADS, HEAD_DIM), 1.0),
        'cache_k': nrm(ks[6], (N_LAYERS_B, n_pool, PAGE_SIZE, B_KV_HEADS, 2 * HEAD_DIM), 1.0),
        'cache_v': nrm(ks[7], (N_LAYERS_B, n_pool, PAGE_SIZE, B_KV_HEADS, B_VDIM), 1.0),
        'page_table': page_table,
        'ada_w': nrm(ks[9], (DEPTH, D_MODEL, N_MOD * D_MODEL), D_MODEL ** -0.5),
        'ada_b': nrm(ks[10], (DEPTH, N_MOD * D_MODEL), 0.02),
        'norm1_w': 1.0 + nrm(ks[11], (DEPTH, D_MODEL), 0.02),
        'norm2_w': 1.0 + nrm(ks[12], (DEPTH, D_MODEL), 0.02),
        'mlp_w1': nrm(ks[13], (DEPTH, D_MODEL, D_FF), D_MODEL ** -0.5),
        'mlp_w2': nrm(ks[14], (DEPTH, D_FF, D_MODEL), D_FF ** -0.5),
        'a_wqkv': nrm(ks[15], (N_LAYERS_A, D_MODEL, a_qkv_w), D_MODEL ** -0.5),
        'a_wo': nrm(ks[16], (N_LAYERS_A, A_HEADS * HEAD_DIM, D_MODEL), (A_HEADS * HEAD_DIM) ** -0.5),
        'a_sinks': nrm(ks[17], (N_LAYERS_A, A_HEADS), 1.0),
        'b_wqkv': nrm(ks[18], (N_LAYERS_B, D_MODEL, b_qkv_w), D_MODEL ** -0.5),
        'b_wo': nrm(ks[19], (N_LAYERS_B, B_HEADS * B_VDIM, D_MODEL), (B_HEADS * B_VDIM) ** -0.5),
        'b_lambda': nrm(ks[20], (N_LAYERS_B, 4, HEAD_DIM), 0.1),
        'b_subln': 1.0 + nrm(ks[21], (N_LAYERS_B, B_VDIM), 0.02),
        'final_norm_w': 1.0 + nrm(ks[22], (D_MODEL,), 0.02),
    }


def reference(x_prompt, x_sample, c_prompt, c_sample, state_win_k, state_win_v, cache_k, cache_v, page_table,
              ada_w, ada_b, norm1_w, norm2_w, mlp_w1, mlp_w2, a_wqkv, a_wo, a_sinks,
              b_wqkv, b_wo, b_lambda, b_subln, final_norm_w):
    yp, ys = x_prompt, x_sample
    wkp, wvp, wks, wvs = [], [], [], []
    kp_rows, vp_rows, ks_rows, vs_rows = [], [], [], []
    for l in range(DEPTH):
        mp = modulation(c_prompt, ada_w[l], ada_b[l])
        ms = modulation(c_sample, ada_w[l], ada_b[l])
        hp = adaln(yp, norm1_w[l], mp[0], mp[1])
        hs = adaln(ys, norm1_w[l], ms[0], ms[1])
        i = l // N_MIXERS
        if l % N_MIXERS == 0:
            op, kp, vp = swa_prompt(hp, a_wqkv[i], a_wo[i], a_sinks[i])
            os_, kn, vn = swa_sample(hs, state_win_k[i], state_win_v[i], a_wqkv[i], a_wo[i], a_sinks[i])
            wkp.append(kp); wvp.append(vp); wks.append(kn); wvs.append(vn)
        else:
            op, kp, vp = diff_prompt(hp, b_wqkv[i], b_wo[i], b_lambda[i], b_subln[i], l)
            os_, kn, vn = diff_sample(hs, cache_k, cache_v, page_table, i, b_wqkv[i], b_wo[i], b_lambda[i], b_subln[i], l)
            kp_rows.append(kp); vp_rows.append(vp); ks_rows.append(kn); vs_rows.append(vn)
        yp = yp + mp[2] * op
        ys = ys + ms[2] * os_
        yp = yp + mp[5] * sq_relu_mlp(adaln(yp, norm2_w[l], mp[3], mp[4]), mlp_w1[l], mlp_w2[l])
        ys = ys + ms[5] * sq_relu_mlp(adaln(ys, norm2_w[l], ms[3], ms[4]), mlp_w1[l], mlp_w2[l])
    yp = rmsnorm(yp, final_norm_w)
    ys = rmsnorm(ys, final_norm_w)
    return (yp, ys,
            jnp.stack(wkp), jnp.stack(wvp), jnp.stack(wks), jnp.stack(wvs),
            jnp.stack(kp_rows), jnp.stack(vp_rows), jnp.stack(ks_rows), jnp.stack(vs_rows))
```

```python
import functools
import math

import jax
import jax.numpy as jnp
from jax import lax
from jax.experimental import pallas as pl
from jax.experimental.pallas import tpu as pltpu

F32 = jnp.float32
BF16 = jnp.bfloat16

D_MODEL = 1024
DEPTH = 4
HEAD_DIM = 64
ROT_HALF = HEAD_DIM // 8
ROPE_THETA = 500000.0
WINDOW = 128
A_HEADS = 16
A_KV_HEADS = 2
B_KV_HEADS = 4
B_VDIM = 128
D_FF = 4 * D_MODEL
RMS_EPS = 1e-5
N_MOD = 6
PAGE_SIZE = 128
SM_SCALE = HEAD_DIM ** -0.5
LANES = 128
VMEM_LIMIT = 56 * 1024 * 1024
NEG_INF = float("-inf")

_NT = (((1,), (1,)), ((), ()))


def _cparams(sem):
    return pltpu.CompilerParams(dimension_semantics=sem, vmem_limit_bytes=VMEM_LIMIT)


def _lambda_init(layer):
    return 0.8 - 0.6 * math.exp(-0.3 * layer)


def _mod_kernel(c_ref, w_ref, b_ref, o_ref):
    a = jax.nn.silu(c_ref[...]).astype(BF16)
    o_ref[...] = jnp.dot(a, w_ref[...].astype(BF16), preferred_element_type=F32) + b_ref[...]


def _modulation(c_all, ada_w, ada_b, tn=1024):
    depth, d, n = ada_w.shape
    rows = c_all.shape[0]
    return pl.pallas_call(
        _mod_kernel,
        grid=(depth, n // tn),
        in_specs=[
            pl.BlockSpec((rows, d), lambda l, j: (0, 0)),
            pl.BlockSpec((None, d, tn), lambda l, j: (l, 0, j)),
            pl.BlockSpec((None, 1, tn), lambda l, j: (l, 0, j)),
        ],
        out_specs=pl.BlockSpec((None, rows, tn), lambda l, j: (l, 0, j)),
        out_shape=jax.ShapeDtypeStruct((depth, rows, n), F32),
        compiler_params=_cparams(("arbitrary", "arbitrary")),
        name="modulation",
    )(c_all, ada_w, ada_b.reshape(depth, 1, n))


def _adaln(x, nw, shift, scale):
    ms = jnp.mean(x * x, axis=-1, keepdims=True)
    return (x * lax.rsqrt(ms + RMS_EPS)) * nw * (1.0 + scale) + shift


def _ln_qkv_kernel(x_ref, sh_ref, sc_ref, nw_ref, w_ref, cos_ref, sa_ref, sb_ref,
                   q_ref, k_ref, v_ref, *, nq, nk):
    x = x_ref[...]
    nb, r, d = x.shape
    h = _adaln(x, nw_ref[...], sh_ref[...], sc_ref[...])
    h2 = h.reshape(nb * r, d).astype(BF16)
    acc = jnp.dot(h2, w_ref[...], preferred_element_type=F32)
    cos, sa, sb = cos_ref[...], sa_ref[...], sb_ref[...]

    def rope(blk):
        return (blk * cos + pltpu.roll(blk, LANES - ROT_HALF, 1) * sa
                + pltpu.roll(blk, ROT_HALF, 1) * sb)

    for j in range(nq // LANES):
        sl = slice(j * LANES, (j + 1) * LANES)
        q_ref[:, sl] = (rope(acc[:, sl]) * SM_SCALE).astype(q_ref.dtype)
    for j in range(nk // LANES):
        k_ref[:, j * LANES:(j + 1) * LANES] = rope(acc[:, nq + j * LANES:nq + (j + 1) * LANES])
    v_ref[...] = acc[:, nq + nk:]


def _rope_tables(pos):
    inv = ROPE_THETA ** (-jnp.arange(ROT_HALF, dtype=F32) / ROT_HALF)
    ang = pos.astype(F32)[:, None] * inv[None, :]
    cos, sin = jnp.cos(ang), jnp.sin(ang)
    dd = jnp.arange(LANES) % HEAD_DIM
    idx = dd % ROT_HALF
    cos_l = jnp.where(dd < 2 * ROT_HALF, cos[:, idx], 1.0)
    sa = jnp.where(dd < ROT_HALF, -sin[:, idx], 0.0)
    sb = jnp.where((dd >= ROT_HALF) & (dd < 2 * ROT_HALF), sin[:, idx], 0.0)
    return cos_l, sa, sb


def _ln_qkv(x3, mods, mod_row0, layer, nw, w, tables, table_tiled, nb, r, nq, nk, q_dtype):
    g, s, d = x3.shape
    n = w.shape[1]
    nv = n - nq - nk
    rows = nb * r
    gs, ss = g // nb, s // r
    t = g * s
    if table_tiled:
        tab_spec = pl.BlockSpec((rows, LANES), lambda i, j: (j, 0))
    else:
        tab_spec = pl.BlockSpec((rows, LANES), lambda i, j: (0, 0))

    def mod_spec(chunk):
        return pl.BlockSpec((None, nb, 1, d), lambda i, j: (layer, mod_row0 // nb + i, 0, chunk))

    def out_spec(width):
        return pl.BlockSpec((rows, width), lambda i, j: (i * ss + j, 0))

    return pl.pallas_call(
        functools.partial(_ln_qkv_kernel, nq=nq, nk=nk),
        grid=(gs, ss),
        in_specs=[
            pl.BlockSpec((nb, r, d), lambda i, j: (i, j, 0)),
            mod_spec(0), mod_spec(1),
            pl.BlockSpec((1, d), lambda i, j: (0, 0)),
            pl.BlockSpec((d, n), lambda i, j: (0, 0)),
            tab_spec, tab_spec, tab_spec,
        ],
        out_specs=[out_spec(nq), out_spec(nk), out_spec(nv)],
        out_shape=[jax.ShapeDtypeStruct((t, nq), q_dtype),
                   jax.ShapeDtypeStruct((t, nk), F32),
                   jax.ShapeDtypeStruct((t, nv), F32)],
        compiler_params=_cparams(("arbitrary", "arbitrary")),
        name="ln_qkv_rope",
    )(x3, mods, mods, nw.reshape(1, d), w, *tables)


def _post_kernel(*refs, final):
    if final:
        (y_ref, o_ref, g1_ref, sh_ref, sc_ref, g2_ref, wo_ref, nw_ref, w1_ref, w2_ref,
         fw_ref, out_ref) = refs
    else:
        (y_ref, o_ref, g1_ref, sh_ref, sc_ref, g2_ref, wo_ref, nw_ref, w1_ref, w2_ref,
         out_ref) = refs
    y = y_ref[...]
    nb, r, d = y.shape
    attn = jnp.dot(o_ref[...].astype(BF16), wo_ref[...], preferred_element_type=F32)
    y1 = y + g1_ref[...] * attn.reshape(nb, r, d)
    h2 = _adaln(y1, nw_ref[...], sh_ref[...], sc_ref[...]).reshape(nb * r, d).astype(BF16)
    t = jnp.dot(h2, w1_ref[...], preferred_element_type=F32)
    t = jnp.square(jnp.maximum(t, 0.0)).astype(BF16)
    mlp = jnp.dot(t, w2_ref[...], preferred_element_type=F32)
    y2 = y1 + g2_ref[...] * mlp.reshape(nb, r, d)
    if final:
        ms = jnp.mean(y2 * y2, axis=-1, keepdims=True)
        y2 = (y2 * lax.rsqrt(ms + RMS_EPS)) * fw_ref[...]
    out_ref[...] = y2


def _post(y3, o2, mods, mod_row0, layer, wo, nw, w1, w2, final_w, nb, r):
    g, s, d = y3.shape
    ff = w1.shape[1]
    rows = nb * r
    gs, ss = g // nb, s // r
    final = final_w is not None

    def mod_spec(chunk):
        return pl.BlockSpec((None, nb, 1, d), lambda i, j: (layer, mod_row0 // nb + i, 0, chunk))

    def const_spec(shape):
        return pl.BlockSpec(shape, lambda i, j: (0, 0), pipeline_mode=pl.Buffered(1))

    in_specs = [
        pl.BlockSpec((nb, r, d), lambda i, j: (i, j, 0)),
        pl.BlockSpec((rows, d), lambda i, j: (i * ss + j, 0)),
        mod_spec(2), mod_spec(3), mod_spec(4), mod_spec(5),
        const_spec((d, d)), const_spec((1, d)), const_spec((d, ff)), const_spec((ff, d)),
    ]
    args = [y3, o2, mods, mods, mods, mods, wo, nw.reshape(1, d), w1, w2]
    if final:
        in_specs.append(const_spec((1, d)))
        args.append(final_w.reshape(1, d))
    return pl.pallas_call(
        functools.partial(_post_kernel, final=final),
        grid=(gs, ss),
        in_specs=in_specs,
        out_specs=pl.BlockSpec((nb, r, d), lambda i, j: (i, j, 0)),
        out_shape=jax.ShapeDtypeStruct((g, s, d), F32),
        compiler_params=_cparams(("arbitrary", "arbitrary")),
        name="proj_mlp",
    )(*args)


def _head_pads(x, x_sw, first):
    lane = lax.broadcasted_iota(jnp.int32, x.shape, 1)
    lo, hi = lane < HEAD_DIM, lane >= HEAD_DIM
    if first:
        return jnp.where(lo, x, 0.0), jnp.where(hi, x_sw, 0.0)
    return jnp.where(lo, x_sw, 0.0), jnp.where(hi, x, 0.0)


def _swa_prompt_kernel(sink_ref, q_ref, kp_ref, kc_ref, vp_ref, vc_ref, o_ref):
    i = pl.program_id(1)
    w = WINDOW
    kk = jnp.concatenate([kp_ref[0], kc_ref[0]], axis=0)
    vv = jnp.concatenate([vp_ref[0], vc_ref[0]], axis=0)
    kk_sw = pltpu.roll(kk, HEAD_DIM, 1)
    vv_sw = pltpu.roll(vv, HEAD_DIM, 1)
    row = lax.broadcasted_iota(jnp.int32, (w, 2 * w), 0)
    col = lax.broadcasted_iota(jnp.int32, (w, 2 * w), 1)
    valid = (col > row) & (col <= row + w) & ((col >= w) | (i > 0))
    pairs_per_kv = A_HEADS // A_KV_HEADS // 2
    for kv in range(A_KV_HEADS):
        k0, k1 = _head_pads(kk, kk_sw, kv == 0)
        v0, v1 = _head_pads(vv, vv_sw, kv == 0)
        kcat = jnp.concatenate([k0, k1], axis=0).astype(BF16)
        vcat = jnp.concatenate([v0, v1], axis=0).astype(BF16)
        for jj in range(pairs_per_kv):
            j = kv * pairs_per_kv + jj
            qp = q_ref[0, :, j * LANES:(j + 1) * LANES]
            s = lax.dot_general(qp, kcat, _NT, preferred_element_type=F32)
            es, dens = [], []
            for par in range(2):
                sc = jnp.where(valid, s[:, par * 2 * w:(par + 1) * 2 * w], NEG_INF)
                sink = sink_ref[2 * j + par]
                m = jnp.maximum(jnp.max(sc, axis=-1, keepdims=True), sink)
                e = jnp.exp(sc - m)
                den = jnp.sum(e, axis=-1, keepdims=True) + jnp.exp(sink - m)
                es.append((e * (1.0 / den)).astype(BF16))
            p = jnp.concatenate(es, axis=1)
            o = jnp.dot(p, vcat, preferred_element_type=F32)
            o_ref[0, :, j * LANES:(j + 1) * LANES] = o.astype(o_ref.dtype)


def _swa_prompt(q3, k3, v3, sinks):
    b, s, dq = q3.shape
    nb = s // WINDOW
    dk = k3.shape[2]
    q_spec = pl.BlockSpec((1, WINDOW, dq), lambda bi, i: (bi, i, 0))
    prev = pl.BlockSpec((1, WINDOW, dk), lambda bi, i: (bi, jnp.maximum(i - 1, 0), 0))
    cur = pl.BlockSpec((1, WINDOW, dk), lambda bi, i: (bi, i, 0))
    return pl.pallas_call(
        _swa_prompt_kernel,
        grid=(b, nb),
        in_specs=[pl.BlockSpec(memory_space=pltpu.SMEM), q_spec, prev, cur, prev, cur],
        out_specs=q_spec,
        out_shape=jax.ShapeDtypeStruct((b, s, dq), BF16),
        compiler_params=_cparams(("arbitrary", "arbitrary")),
        name="swa_prompt",
    )(sinks, q3, k3, k3, v3, v3)


def _swa_sample_kernel(sink_ref, q_ref, kn_ref, vn_ref, bk_ref, bv_ref, o_ref, wk_ref, wv_ref):
    sb, t, _ = q_ref.shape
    wb = bk_ref.shape[1]
    nkeys = 2 * wb
    pairs_per_kv = A_HEADS // A_KV_HEADS // 2
    rows = pairs_per_kv * t
    row_t = lax.broadcasted_iota(jnp.int32, (rows, nkeys), 0) % t
    col = lax.broadcasted_iota(jnp.int32, (rows, nkeys), 1)
    valid = (col > row_t + wb - WINDOW) & (col <= row_t + wb)
    row_pair = lax.broadcasted_iota(jnp.int32, (rows, 1), 0) // t
    pad = jnp.zeros((wb - t, LANES), F32)

    def body(si, carry):
        kn, vn, bk, bv = kn_ref[si], vn_ref[si], bk_ref[si], bv_ref[si]
        wk_ref[si, 0:wb - t, :] = bk[t:, :]
        wk_ref[si, wb - t:wb, :] = kn
        wv_ref[si, 0:wb - t, :] = bv[t:, :]
        wv_ref[si, wb - t:wb, :] = vn
        kk = jnp.concatenate([bk, kn, pad], axis=0)
        vv = jnp.concatenate([bv, vn, pad], axis=0)
        kk_sw = pltpu.roll(kk, HEAD_DIM, 1)
        vv_sw = pltpu.roll(vv, HEAD_DIM, 1)
        q = q_ref[si]
        for kv in range(A_KV_HEADS):
            kpads = _head_pads(kk, kk_sw, kv == 0)
            vpads = _head_pads(vv, vv_sw, kv == 0)
            qs = jnp.concatenate(
                [q[:, (kv * pairs_per_kv + jj) * LANES:(kv * pairs_per_kv + jj + 1) * LANES]
                 for jj in range(pairs_per_kv)], axis=0).astype(BF16)
            o = jnp.zeros((rows, LANES), F32)
            for par in range(2):
                sink = jnp.zeros((rows, 1), F32)
                for jj in range(pairs_per_kv):
                    head = 2 * (kv * pairs_per_kv + jj) + par
                    sink = jnp.where(row_pair == jj, sink_ref[head], sink)
                sc = lax.dot_general(qs, kpads[par].astype(BF16), _NT, preferred_element_type=F32)
                sc = jnp.where(valid, sc, NEG_INF)
                m = jnp.maximum(jnp.max(sc, axis=-1, keepdims=True), sink)
                e = jnp.exp(sc - m)
                den = jnp.sum(e, axis=-1, keepdims=True) + jnp.exp(sink - m)
                o = o + jnp.dot((e * (1.0 / den)).astype(BF16), vpads[par].astype(BF16),
                                preferred_element_type=F32)
            for jj in range(pairs_per_kv):
                j = kv * pairs_per_kv + jj
                o_ref[si, :, j * LANES:(j + 1) * LANES] = o[jj * t:(jj + 1) * t, :]
        return carry

    lax.fori_loop(0, sb, body, 0)


def _swa_sample(q3, kn3, vn3, bk3, bv3, sinks, sb=16):
    b, t, dq = q3.shape
    wb, dk = bk3.shape[1], bk3.shape[2]

    def spec(rows, width):
        return pl.BlockSpec((sb, rows, width), lambda i: (i, 0, 0))

    return pl.pallas_call(
        _swa_sample_kernel,
        grid=(b // sb,),
        in_specs=[pl.BlockSpec(memory_space=pltpu.SMEM), spec(t, dq), spec(t, dk), spec(t, dk),
                  spec(wb, dk), spec(wb, dk)],
        out_specs=[spec(t, dq), spec(wb, dk), spec(wb, dk)],
        out_shape=[jax.ShapeDtypeStruct((b, t, dq), F32),
                   jax.ShapeDtypeStruct((b, wb, dk), F32),
                   jax.ShapeDtypeStruct((b, wb, dk), F32)],
        compiler_params=_cparams(("arbitrary",)),
        name="swa_sample",
    )(sinks, q3, kn3, vn3, bk3, bv3)


def _diff_lambda(lam_ref, layer):
    lf = lam_ref[...]
    s1 = jnp.sum(lf[0:1, :] * lf[1:2, :], axis=-1, keepdims=True)
    s2 = jnp.sum(lf[2:3, :] * lf[3:4, :], axis=-1, keepdims=True)
    return jnp.exp(s1) - jnp.exp(s2) + _lambda_init(layer)


def _diff_combine(acc0, l0, acc1, l1, lam, sub, layer):
    o = acc0 * (1.0 / l0) - lam * (acc1 * (1.0 / l1))
    ms = jnp.mean(o * o, axis=-1, keepdims=True)
    return (o * lax.rsqrt(ms + RMS_EPS)) * sub * (1.0 - _lambda_init(layer))


def _online_update(sc, v, m_ref, l_ref, acc_ref, idx):
    m_prev = m_ref[idx]
    m_new = jnp.maximum(m_prev, jnp.max(sc, axis=-1, keepdims=True))
    alpha = jnp.exp(m_prev - m_new)
    e = jnp.exp(sc - m_new)
    l_ref[idx] = alpha * l_ref[idx] + jnp.sum(e, axis=-1, keepdims=True)
    acc_ref[idx] = alpha * acc_ref[idx] + jnp.dot(e.astype(BF16), v, preferred_element_type=F32)
    m_ref[idx] = m_new


def _diff_prompt_kernel(qi_ref, ki_ref, q_ref, k_ref, v_ref, lam_ref, sub_ref, o_ref,
                        m_ref, l_ref, acc_ref, *, layer):
    step = pl.program_id(2)
    qi, ki = qi_ref[step], ki_ref[step]
    tq = q_ref.shape[1]
    tk = k_ref.shape[1]

    @pl.when(ki == 0)
    def _():
        m_ref[...] = jnp.full(m_ref.shape, NEG_INF, F32)
        l_ref[...] = jnp.zeros(l_ref.shape, F32)
        acc_ref[...] = jnp.zeros(acc_ref.shape, F32)

    def update(masked):
        q = q_ref[0]
        qs = jnp.concatenate([q[:, :LANES], q[:, LANES:]], axis=0)
        k = k_ref[0]
        lane = lax.broadcasted_iota(jnp.int32, k.shape, 1)
        kcat = jnp.concatenate([jnp.where(lane < HEAD_DIM, k, 0.0),
                                jnp.where(lane >= HEAD_DIM, k, 0.0)], axis=0).astype(BF16)
        s = lax.dot_general(qs, kcat, _NT, preferred_element_type=F32)
        v = v_ref[0].astype(BF16)
        if masked:
            row = lax.broadcasted_iota(jnp.int32, (2 * tq, tk), 0) % tq
            col = lax.broadcasted_iota(jnp.int32, (2 * tq, tk), 1)
            causal = row >= col
        for c in range(2):
            sc = s[:, c * tk:(c + 1) * tk]
            if masked:
                sc = jnp.where(causal, sc, NEG_INF)
            _online_update(sc, v, m_ref, l_ref, acc_ref, c)

    @pl.when(ki < qi)
    def _():
        update(False)

    @pl.when(ki == qi)
    def _():
        update(True)
        lam = _diff_lambda(lam_ref, layer)
        o = _diff_combine(acc_ref[0], l_ref[0], acc_ref[1], l_ref[1], lam, sub_ref[...], layer)
        o_ref[0, :, 0:LANES] = o[:tq].astype(o_ref.dtype)
        o_ref[0, :, LANES:2 * LANES] = o[tq:].astype(o_ref.dtype)


def _diff_prompt(q3, k3, v3, lam_p, subln, layer, tile=512):
    b, s, dq = q3.shape
    nkv = k3.shape[2] // LANES
    tile = min(tile, s)
    nt = s // tile
    qi = jnp.asarray([i for i in range(nt) for _ in range(i + 1)], jnp.int32)
    ki = jnp.asarray([j for i in range(nt) for j in range(i + 1)], jnp.int32)
    grid_spec = pltpu.PrefetchScalarGridSpec(
        num_scalar_prefetch=2,
        grid=(b, nkv, int(qi.shape[0])),
        in_specs=[
            pl.BlockSpec((1, tile, 2 * LANES), lambda bi, h, t, qr, kr: (bi, qr[t], h)),
            pl.BlockSpec((1, tile, LANES), lambda bi, h, t, qr, kr: (bi, kr[t], h)),
            pl.BlockSpec((1, tile, LANES), lambda bi, h, t, qr, kr: (bi, kr[t], h)),
            pl.BlockSpec(lam_p.shape, lambda bi, h, t, qr, kr: (0, 0)),
            pl.BlockSpec((1, B_VDIM), lambda bi, h, t, qr, kr: (0, 0)),
        ],
        out_specs=pl.BlockSpec((1, tile, 2 * LANES), lambda bi, h, t, qr, kr: (bi, qr[t], h)),
        scratch_shapes=[pltpu.VMEM((2, 2 * tile, 1), F32), pltpu.VMEM((2, 2 * tile, 1), F32),
                        pltpu.VMEM((2, 2 * tile, B_VDIM), F32)],
    )
    return pl.pallas_call(
        functools.partial(_diff_prompt_kernel, layer=layer),
        grid_spec=grid_spec,
        out_shape=jax.ShapeDtypeStruct((b, s, dq), BF16),
        compiler_params=_cparams(("arbitrary", "arbitrary", "arbitrary")),
        name="diff_prompt",
    )(qi, ki, q3, k3, v3, lam_p, subln.reshape(1, B_VDIM))


def _diff_sample_kernel(pt_ref, q_ref, kn_ref, vn_ref, lam_ref, sub_ref, *rest, pages, layer):
    k_refs, v_refs = rest[:pages], rest[pages:2 * pages]
    o_ref = rest[2 * pages]
    wt_ref, kb_ref, vb_ref, m_ref, l_ref, acc_ref = rest[2 * pages + 1:]
    step = pl.program_id(1)
    t = q_ref.shape[1]
    grp = 2 * t

    @pl.when(step == 0)
    def _():
        wt_ref[...] = jnp.zeros(wt_ref.shape, BF16)
        q = q_ref[0]
        lane = lax.broadcasted_iota(jnp.int32, (grp, LANES), 1)
        for kv in range(B_KV_HEADS):
            qq = jnp.concatenate([q[:, (2 * kv) * LANES:(2 * kv + 1) * LANES],
                                  q[:, (2 * kv + 1) * LANES:(2 * kv + 2) * LANES]], axis=0)
            r0 = kv * 2 * grp
            wt_ref[r0:r0 + grp, kv * LANES:(kv + 1) * LANES] = (
                jnp.where(lane < HEAD_DIM, qq, 0.0).astype(BF16))
            wt_ref[r0 + grp:r0 + 2 * grp, kv * LANES:(kv + 1) * LANES] = (
                jnp.where(lane >= HEAD_DIM, qq, 0.0).astype(BF16))
        m_ref[...] = jnp.full(m_ref.shape, NEG_INF, F32)
        l_ref[...] = jnp.zeros(l_ref.shape, F32)
        acc_ref[...] = jnp.zeros(acc_ref.shape, F32)

    for p in range(pages):
        for kv in range(B_KV_HEADS):
            rows = pl.ds(kv, PAGE_SIZE, stride=B_KV_HEADS)
            dst = (slice(p * PAGE_SIZE, (p + 1) * PAGE_SIZE), slice(kv * LANES, (kv + 1) * LANES))
            kb_ref[dst] = k_refs[p][rows, :].astype(BF16)
            vb_ref[dst] = v_refs[p][rows, :].astype(BF16)
    wt = wt_ref[...]
    sc = lax.dot_general(wt, kb_ref[...], _NT, preferred_element_type=F32)
    _online_update(sc, vb_ref[...], m_ref, l_ref, acc_ref, 0)

    @pl.when(step == pl.num_programs(1) - 1)
    def _():
        nrows = wt.shape[0]
        padz = jnp.zeros((LANES - t, kn_ref.shape[2]), F32)
        knb = jnp.concatenate([kn_ref[0], padz], axis=0).astype(BF16)
        vnb = jnp.concatenate([vn_ref[0], padz], axis=0).astype(BF16)
        sn = lax.dot_general(wt, knb, _NT, preferred_element_type=F32)
        row_t = lax.broadcasted_iota(jnp.int32, (nrows, LANES), 0) % t
        col = lax.broadcasted_iota(jnp.int32, (nrows, LANES), 1)
        sn = jnp.where((col < t) & (row_t >= col), sn, NEG_INF)
        _online_update(sn, vnb, m_ref, l_ref, acc_ref, 0)
        lam = _diff_lambda(lam_ref, layer)
        acc, l = acc_ref[0], l_ref[0]
        for kv in range(B_KV_HEADS):
            r0 = kv * 2 * grp
            ls = slice(kv * LANES, (kv + 1) * LANES)
            o = _diff_combine(acc[r0:r0 + grp, ls], l[r0:r0 + grp],
                              acc[r0 + grp:r0 + 2 * grp, ls], l[r0 + grp:r0 + 2 * grp],
                              lam, sub_ref[...], layer)
            o_ref[0, :, (2 * kv) * LANES:(2 * kv + 1) * LANES] = o[:t]
            o_ref[0, :, (2 * kv + 1) * LANES:(2 * kv + 2) * LANES] = o[t:]


def _diff_sample(q3, kn3, vn3, cache_k4, cache_v4, page_table, cache_layer, lam_p, subln,
                 layer, pages=8):
    b, t, dq = q3.shape
    n_pages = page_table.shape[1]
    pages = min(pages, n_pages)
    dk = B_KV_HEADS * cache_k4.shape[3]
    nrows = B_KV_HEADS * 2 * 2 * t
    pt_flat = page_table.reshape(-1)

    def page_spec(p):
        return pl.BlockSpec(
            (None, None, PAGE_SIZE * B_KV_HEADS, cache_k4.shape[3]),
            lambda bi, s, pt: (cache_layer, pt[bi * n_pages + s * pages + p], 0, 0))

    seq_spec = lambda width: pl.BlockSpec((1, t, width), lambda bi, s, pt: (bi, 0, 0))
    grid_spec = pltpu.PrefetchScalarGridSpec(
        num_scalar_prefetch=1,
        grid=(b, n_pages // pages),
        in_specs=[seq_spec(dq), seq_spec(dk), seq_spec(dk),
                  pl.BlockSpec(lam_p.shape, lambda bi, s, pt: (0, 0)),
                  pl.BlockSpec((1, B_VDIM), lambda bi, s, pt: (0, 0))]
                 + [page_spec(p) for p in range(pages)] + [page_spec(p) for p in range(pages)],
        out_specs=seq_spec(dq),
        scratch_shapes=[pltpu.VMEM((nrows, dk), BF16),
                        pltpu.VMEM((pages * PAGE_SIZE, dk), BF16),
                        pltpu.VMEM((pages * PAGE_SIZE, dk), BF16),
                        pltpu.VMEM((1, nrows, 1), F32), pltpu.VMEM((1, nrows, 1), F32),
                        pltpu.VMEM((1, nrows, dk), F32)],
    )
    return pl.pallas_call(
        functools.partial(_diff_sample_kernel, pages=pages, layer=layer),
        grid_spec=grid_spec,
        out_shape=jax.ShapeDtypeStruct((b, t, dq), F32),
        compiler_params=_cparams(("arbitrary", "arbitrary")),
        name="diff_sample",
    )(pt_flat, q3, kn3, vn3, lam_p, subln.reshape(1, B_VDIM),
      *([cache_k4] * pages), *([cache_v4] * pages))


def _trunk(x_prompt, x_sample, c_prompt, c_sample, state_win_k, state_win_v, cache_k, cache_v,
           page_table, ada_w, ada_b, norm1_w, norm2_w, mlp_w1, mlp_w2, a_wqkv, a_wo, a_sinks,
           b_wqkv, b_wo, b_lambda, b_subln, final_norm_w, past_len, prompt_rows, sample_seqs):
    bp, sp, d = x_prompt.shape
    bs, ts, _ = x_sample.shape
    depth = ada_w.shape[0]
    c_all = jnp.concatenate([c_sample, c_prompt], axis=0)
    mods = _modulation(c_all, ada_w, ada_b).reshape(depth, bs + bp, 1, N_MOD * d)
    tab_p = _rope_tables(jnp.arange(sp))
    tab_s = tuple(jnp.tile(tb, (sample_seqs, 1)) for tb in _rope_tables(past_len + jnp.arange(ts)))

    a_wqkv, a_wo, b_wqkv, b_wo = (w.astype(BF16) for w in (a_wqkv, a_wo, b_wqkv, b_wo))
    mlp_w1, mlp_w2 = mlp_w1.astype(BF16), mlp_w2.astype(BF16)
    n_pool = cache_k.shape[1]
    cache_k4 = cache_k.reshape(cache_k.shape[0], n_pool, PAGE_SIZE * B_KV_HEADS, -1)
    cache_v4 = cache_v.reshape(cache_v.shape[0], n_pool, PAGE_SIZE * B_KV_HEADS, -1)

    yp, ys = x_prompt, x_sample
    wkp, wvp, wks, wvs, kps, vps, kss, vss = ([] for _ in range(8))
    for l in range(depth):
        i = l // 2
        mixer_a = l % 2 == 0
        wqkv = a_wqkv[i] if mixer_a else b_wqkv[i]
        wo = a_wo[i] if mixer_a else b_wo[i]
        nq = d
        nk = (A_KV_HEADS * HEAD_DIM) if mixer_a else (B_KV_HEADS * 2 * HEAD_DIM)
        qp, kp, vp = _ln_qkv(yp, mods, bs, l, norm1_w[l], wqkv, tab_p, True,
                             1, prompt_rows, nq, nk, BF16)
        qs, ks, vs = _ln_qkv(ys, mods, 0, l, norm1_w[l], wqkv, tab_s, False,
                             sample_seqs, ts, nq, nk, F32)
        qp3, kp3, vp3 = (a.reshape(bp, sp, -1) for a in (qp, kp, vp))
        qs3, ks3, vs3 = (a.reshape(bs, ts, -1) for a in (qs, ks, vs))
        if mixer_a:
            op = _swa_prompt(qp3, kp3, vp3, a_sinks[i])
            wb = state_win_k.shape[2]
            os_, wk, wv = _swa_sample(qs3, ks3, vs3, state_win_k[i].reshape(bs, wb, -1),
                                      state_win_v[i].reshape(bs, wb, -1), a_sinks[i])
            keep = min(WINDOW, sp)
            wkp.append(kp3[:, sp - keep:].reshape(bp, keep, A_KV_HEADS, HEAD_DIM))
            wvp.append(vp3[:, sp - keep:].reshape(bp, keep, A_KV_HEADS, HEAD_DIM))
            wks.append(wk.reshape(bs, wb, A_KV_HEADS, HEAD_DIM))
            wvs.append(wv.reshape(bs, wb, A_KV_HEADS, HEAD_DIM))
        else:
            op = _diff_prompt(qp3, kp3, vp3, b_lambda[i], b_subln[i], l)
            os_ = _diff_sample(qs3, ks3, vs3, cache_k4, cache_v4, page_table, i,
                               b_lambda[i], b_subln[i], l)
            kps.append(kp3.reshape(bp, sp, B_KV_HEADS, 2 * HEAD_DIM))
            vps.append(vp3.reshape(bp, sp, B_KV_HEADS, B_VDIM))
            kss.append(ks3.reshape(bs, ts, B_KV_HEADS, 2 * HEAD_DIM))
            vss.append(vs3.reshape(bs, ts, B_KV_HEADS, B_VDIM))
        fw = final_norm_w if l == depth - 1 else None
        yp = _post(yp, op.reshape(bp * sp, d), mods, bs, l, wo, norm2_w[l], mlp_w1[l], mlp_w2[l],
                   fw, 1, prompt_rows)
        ys = _post(ys, os_.reshape(bs * ts, d), mods, 0, l, wo, norm2_w[l], mlp_w1[l], mlp_w2[l],
                   fw, sample_seqs, ts)
    return (yp, ys, jnp.stack(wkp), jnp.stack(wvp), jnp.stack(wks), jnp.stack(wvs),
            jnp.stack(kps), jnp.stack(vps), jnp.stack(kss), jnp.stack(vss))


def kernel(x_prompt, x_sample, c_prompt, c_sample, state_win_k, state_win_v, cache_k, cache_v,
           page_table, ada_w, ada_b, norm1_w, norm2_w, mlp_w1, mlp_w2, a_wqkv, a_wo, a_sinks,
           b_wqkv, b_wo, b_lambda, b_subln, final_norm_w):
    past_len = page_table.shape[1] * PAGE_SIZE
    return _trunk(x_prompt, x_sample, c_prompt, c_sample, state_win_k, state_win_v, cache_k,
                  cache_v, page_table, ada_w, ada_b, norm1_w, norm2_w, mlp_w1, mlp_w2, a_wqkv,
                  a_wo, a_sinks, b_wqkv, b_wo, b_lambda, b_subln, final_norm_w,
                  past_len=past_len, prompt_rows=min(512, x_prompt.shape[1]),
                  sample_seqs=min(64, x_sample.shape[0]))
```

```python
import functools
import math

import jax
import jax.numpy as jnp
from jax import lax
from jax.experimental import pallas as pl
from jax.experimental.pallas import tpu as pltpu

F32 = jnp.float32
BF16 = jnp.bfloat16

D_MODEL = 1024
DEPTH = 4
HEAD_DIM = 64
ROT_HALF = HEAD_DIM // 8
ROPE_THETA = 500000.0
WINDOW = 128
A_HEADS = 16
A_KV_HEADS = 2
B_KV_HEADS = 4
B_VDIM = 128
D_FF = 4 * D_MODEL
RMS_EPS = 1e-5
N_MOD = 6
PAGE_SIZE = 128
SM_SCALE = HEAD_DIM ** -0.5
SM_SCALE_LOG2 = SM_SCALE * math.log2(math.e)
LANES = 128
VMEM_LIMIT = 56 * 1024 * 1024
NEG_INF = float("-inf")

_NT = (((1,), (1,)), ((), ()))


def _cparams(sem):
    return pltpu.CompilerParams(dimension_semantics=sem, vmem_limit_bytes=VMEM_LIMIT)


def _lambda_init(layer):
    return 0.8 - 0.6 * math.exp(-0.3 * layer)


def _mod_kernel(c_ref, w_ref, b_ref, o_ref):
    a = jax.nn.silu(c_ref[...]).astype(BF16)
    o_ref[...] = jnp.dot(a, w_ref[...].astype(BF16), preferred_element_type=F32) + b_ref[...]


def _modulation(c_all, ada_w, ada_b, tn=1024):
    depth, d, n = ada_w.shape
    rows = c_all.shape[0]
    return pl.pallas_call(
        _mod_kernel,
        grid=(depth, n // tn),
        in_specs=[
            pl.BlockSpec((rows, d), lambda l, j: (0, 0)),
            pl.BlockSpec((None, d, tn), lambda l, j: (l, 0, j)),
            pl.BlockSpec((None, 1, tn), lambda l, j: (l, 0, j)),
        ],
        out_specs=pl.BlockSpec((None, rows, tn), lambda l, j: (l, 0, j)),
        out_shape=jax.ShapeDtypeStruct((depth, rows, n), F32),
        compiler_params=_cparams(("arbitrary", "arbitrary")),
        name="modulation",
    )(c_all, ada_w, ada_b.reshape(depth, 1, n))


def _adaln(x, nw, shift, scale):
    ms = jnp.mean(x * x, axis=-1, keepdims=True)
    return (x * lax.rsqrt(ms + RMS_EPS)) * nw * (1.0 + scale) + shift


def _ln_qkv_kernel(x_ref, sh_ref, sc_ref, nw_ref, w_ref, cos_ref, sa_ref, sb_ref, *rest,
                   nq, nk, q_scale, interleave):
    q_ref, k_ref, v_ref = rest[-3:]
    x = x_ref[...]
    nb, r, d = x.shape
    h = _adaln(x, nw_ref[...], sh_ref[...], sc_ref[...])
    h2 = h.reshape(nb * r, d).astype(BF16)
    acc = jnp.dot(h2, w_ref[...], preferred_element_type=F32)
    cos, sa, sb = cos_ref[...], sa_ref[...], sb_ref[...]

    def rope(blk):
        return (blk * cos + pltpu.roll(blk, LANES - ROT_HALF, 1) * sa
                + pltpu.roll(blk, ROT_HALF, 1) * sb)

    rows = nb * r
    for j in range(nq // LANES):
        sl = slice(j * LANES, (j + 1) * LANES)
        q_ref[:, sl] = (rope(acc[:, sl]) * q_scale).astype(q_ref.dtype)
    hk = nk // LANES
    hv = (acc.shape[1] - nq - nk) // LANES
    for j in range(hk):
        kj = rope(acc[:, nq + j * LANES:nq + (j + 1) * LANES])
        if interleave:
            k_ref[pl.ds(j, rows, stride=hk), :] = kj
        else:
            k_ref[:, j * LANES:(j + 1) * LANES] = kj
    if interleave:
        for j in range(hv):
            v_ref[pl.ds(j, rows, stride=hv), :] = acc[:, nq + nk + j * LANES:nq + nk + (j + 1) * LANES]
    else:
        v_ref[...] = acc[:, nq + nk:]


def _rope_tables(pos):
    inv = ROPE_THETA ** (-jnp.arange(ROT_HALF, dtype=F32) / ROT_HALF)
    ang = pos.astype(F32)[:, None] * inv[None, :]
    cos, sin = jnp.cos(ang), jnp.sin(ang)
    dd = jnp.arange(LANES) % HEAD_DIM
    idx = dd % ROT_HALF
    cos_l = jnp.where(dd < 2 * ROT_HALF, cos[:, idx], 1.0)
    sa = jnp.where(dd < ROT_HALF, -sin[:, idx], 0.0)
    sb = jnp.where((dd >= ROT_HALF) & (dd < 2 * ROT_HALF), sin[:, idx], 0.0)
    return cos_l, sa, sb


def _ln_qkv(x3, mods, mod_row0, layer, nw, w, tables, table_tiled, nb, r, nq, nk, q_dtype,
            q_scale, kv_slot=None):
    g, s, d = x3.shape
    n = w.shape[1]
    nv = n - nq - nk
    rows = nb * r
    gs, ss = g // nb, s // r
    t = g * s
    if table_tiled:
        tab_spec = pl.BlockSpec((rows, LANES), lambda i, j: (j, 0))
    else:
        tab_spec = pl.BlockSpec((rows, LANES), lambda i, j: (0, 0))

    def mod_spec(chunk):
        return pl.BlockSpec((None, nb, 1, d), lambda i, j: (layer, mod_row0 // nb + i, 0, chunk))

    def out_spec(width):
        return pl.BlockSpec((rows, width), lambda i, j: (i * ss + j, 0))

    in_specs = [
        pl.BlockSpec((nb, r, d), lambda i, j: (i, j, 0)),
        mod_spec(0), mod_spec(1),
        pl.BlockSpec((1, d), lambda i, j: (0, 0)),
        pl.BlockSpec((d, n), lambda i, j: (0, 0)),
        tab_spec, tab_spec, tab_spec,
    ]
    args = [x3, mods, mods, nw.reshape(1, d), w, *tables]
    aliases = {}
    if kv_slot is None:
        kv_specs = [out_spec(nk), out_spec(nv)]
        kv_shapes = [jax.ShapeDtypeStruct((t, nk), F32), jax.ShapeDtypeStruct((t, nv), F32)]
    else:
        slot, n_slots, k_buf, v_buf = kv_slot
        hk, hv = nk // LANES, nv // LANES

        def slot_spec(heads):
            return pl.BlockSpec((None, rows * heads, LANES), lambda i, j: (slot, i * ss + j, 0))

        kv_specs = [slot_spec(hk), slot_spec(hv)]
        kv_shapes = [jax.ShapeDtypeStruct((n_slots, t * hk, LANES), F32),
                     jax.ShapeDtypeStruct((n_slots, t * hv, LANES), F32)]
        if k_buf is not None:
            aliases = {len(args): 1, len(args) + 1: 2}
            in_specs += [pl.BlockSpec(memory_space=pl.ANY)] * 2
            args += [k_buf, v_buf]
    return pl.pallas_call(
        functools.partial(_ln_qkv_kernel, nq=nq, nk=nk, q_scale=q_scale,
                          interleave=kv_slot is not None),
        grid=(gs, ss),
        in_specs=in_specs,
        out_specs=[out_spec(nq)] + kv_specs,
        out_shape=[jax.ShapeDtypeStruct((t, nq), q_dtype)] + kv_shapes,
        input_output_aliases=aliases,
        compiler_params=_cparams(("arbitrary", "arbitrary")),
        name="ln_qkv_rope",
    )(*args)


def _post_kernel(*refs, final):
    if final:
        (y_ref, o_ref, g1_ref, sh_ref, sc_ref, g2_ref, wo_ref, nw_ref, w1_ref, w2_ref,
         fw_ref, out_ref) = refs
    else:
        (y_ref, o_ref, g1_ref, sh_ref, sc_ref, g2_ref, wo_ref, nw_ref, w1_ref, w2_ref,
         out_ref) = refs
    y = y_ref[...]
    nb, r, d = y.shape
    attn = jnp.dot(o_ref[...].astype(BF16), wo_ref[...], preferred_element_type=F32)
    y1 = y + g1_ref[...] * attn.reshape(nb, r, d)
    h2 = _adaln(y1, nw_ref[...], sh_ref[...], sc_ref[...]).reshape(nb * r, d).astype(BF16)
    t = jnp.dot(h2, w1_ref[...], preferred_element_type=F32)
    t = jnp.square(jnp.maximum(t, 0.0)).astype(BF16)
    mlp = jnp.dot(t, w2_ref[...], preferred_element_type=F32)
    y2 = y1 + g2_ref[...] * mlp.reshape(nb, r, d)
    if final:
        ms = jnp.mean(y2 * y2, axis=-1, keepdims=True)
        y2 = (y2 * lax.rsqrt(ms + RMS_EPS)) * fw_ref[...]
    out_ref[...] = y2


def _post(y3, o2, mods, mod_row0, layer, wo, nw, w1, w2, final_w, nb, r):
    g, s, d = y3.shape
    ff = w1.shape[1]
    rows = nb * r
    gs, ss = g // nb, s // r
    final = final_w is not None

    def mod_spec(chunk):
        return pl.BlockSpec((None, nb, 1, d), lambda i, j: (layer, mod_row0 // nb + i, 0, chunk))

    def const_spec(shape):
        return pl.BlockSpec(shape, lambda i, j: (0, 0), pipeline_mode=pl.Buffered(1))

    in_specs = [
        pl.BlockSpec((nb, r, d), lambda i, j: (i, j, 0)),
        pl.BlockSpec((rows, d), lambda i, j: (i * ss + j, 0)),
        mod_spec(2), mod_spec(3), mod_spec(4), mod_spec(5),
        const_spec((d, d)), const_spec((1, d)), const_spec((d, ff)), const_spec((ff, d)),
    ]
    args = [y3, o2, mods, mods, mods, mods, wo, nw.reshape(1, d), w1, w2]
    if final:
        in_specs.append(const_spec((1, d)))
        args.append(final_w.reshape(1, d))
    return pl.pallas_call(
        functools.partial(_post_kernel, final=final),
        grid=(gs, ss),
        in_specs=in_specs,
        out_specs=pl.BlockSpec((nb, r, d), lambda i, j: (i, j, 0)),
        out_shape=jax.ShapeDtypeStruct((g, s, d), F32),
        compiler_params=_cparams(("arbitrary", "arbitrary")),
        name="proj_mlp",
    )(*args)


def _head_pads(x, x_sw, first):
    lane = lax.broadcasted_iota(jnp.int32, x.shape, 1)
    lo, hi = lane < HEAD_DIM, lane >= HEAD_DIM
    if first:
        return jnp.where(lo, x, 0.0), jnp.where(hi, x_sw, 0.0)
    return jnp.where(lo, x_sw, 0.0), jnp.where(hi, x, 0.0)


def _swa_prompt_kernel(sink_ref, q_ref, kp_ref, kc_ref, vp_ref, vc_ref, o_ref):
    i = pl.program_id(1)
    w = WINDOW
    kk = jnp.concatenate([kp_ref[0], kc_ref[0]], axis=0)
    vv = jnp.concatenate([vp_ref[0], vc_ref[0]], axis=0)
    kk_sw = pltpu.roll(kk, HEAD_DIM, 1)
    vv_sw = pltpu.roll(vv, HEAD_DIM, 1)
    row = lax.broadcasted_iota(jnp.int32, (w, 2 * w), 0)
    col = lax.broadcasted_iota(jnp.int32, (w, 2 * w), 1)
    valid = (col > row) & (col <= row + w) & ((col >= w) | (i > 0))
    pairs_per_kv = A_HEADS // A_KV_HEADS // 2
    npairs = A_HEADS // 2
    halves = range(2)
    kcats, vcats = [], []
    for kv in range(A_KV_HEADS):
        k0, k1 = _head_pads(kk, kk_sw, kv == 0)
        v0, v1 = _head_pads(vv, vv_sw, kv == 0)
        kcats.append(jnp.concatenate([k0, k1], axis=0).astype(BF16))
        vcats.append(jnp.concatenate([v0, v1], axis=0).astype(BF16))
    ss = [lax.dot_general(q_ref[0, :, j * LANES:(j + 1) * LANES], kcats[j // pairs_per_kv], _NT,
                          preferred_element_type=F32) for j in range(npairs)]
    scs = [[jnp.where(valid, ss[j][:, par * 2 * w:(par + 1) * 2 * w], NEG_INF) for par in halves]
           for j in range(npairs)]
    ms = [[jnp.maximum(jnp.max(scs[j][par], axis=-1, keepdims=True), sink_ref[2 * j + par])
           for par in halves] for j in range(npairs)]
    es = [[jnp.exp(scs[j][par] - ms[j][par]) for par in halves] for j in range(npairs)]
    dens = [[jnp.sum(es[j][par], axis=-1, keepdims=True)
             + jnp.exp(sink_ref[2 * j + par] - ms[j][par]) for par in halves]
            for j in range(npairs)]
    ps = [jnp.concatenate([(es[j][par] * (1.0 / dens[j][par])).astype(BF16) for par in halves],
                          axis=1) for j in range(npairs)]
    for j in range(npairs):
        o = jnp.dot(ps[j], vcats[j // pairs_per_kv], preferred_element_type=F32)
        o_ref[0, :, j * LANES:(j + 1) * LANES] = o.astype(o_ref.dtype)


def _swa_prompt(q3, k3, v3, sinks):
    b, s, dq = q3.shape
    nb = s // WINDOW
    dk = k3.shape[2]
    q_spec = pl.BlockSpec((1, WINDOW, dq), lambda bi, i: (bi, i, 0))
    prev = pl.BlockSpec((1, WINDOW, dk), lambda bi, i: (bi, jnp.maximum(i - 1, 0), 0))
    cur = pl.BlockSpec((1, WINDOW, dk), lambda bi, i: (bi, i, 0))
    return pl.pallas_call(
        _swa_prompt_kernel,
        grid=(b, nb),
        in_specs=[pl.BlockSpec(memory_space=pltpu.SMEM), q_spec, prev, cur, prev, cur],
        out_specs=q_spec,
        out_shape=jax.ShapeDtypeStruct((b, s, dq), BF16),
        compiler_params=_cparams(("arbitrary", "arbitrary")),
        name="swa_prompt",
    )(sinks, q3, k3, k3, v3, v3)


def _swa_sample_kernel(sink_ref, q_ref, kn_ref, vn_ref, bk_ref, bv_ref, o_ref, wk_ref, wv_ref):
    sb, t, _ = q_ref.shape
    wb = bk_ref.shape[1]
    nkeys = 2 * wb
    pairs_per_kv = A_HEADS // A_KV_HEADS // 2
    rows = pairs_per_kv * t
    row_t = lax.broadcasted_iota(jnp.int32, (rows, nkeys), 0) % t
    col = lax.broadcasted_iota(jnp.int32, (rows, nkeys), 1)
    valid = (col > row_t + wb - WINDOW) & (col <= row_t + wb)
    row_pair = lax.broadcasted_iota(jnp.int32, (rows, 1), 0) // t
    pad = jnp.zeros((wb - t, LANES), F32)

    def body(si, carry):
        kn, vn, bk, bv = kn_ref[si], vn_ref[si], bk_ref[si], bv_ref[si]
        wk_ref[si, 0:wb - t, :] = bk[t:, :]
        wk_ref[si, wb - t:wb, :] = kn
        wv_ref[si, 0:wb - t, :] = bv[t:, :]
        wv_ref[si, wb - t:wb, :] = vn
        kk = jnp.concatenate([bk, kn, pad], axis=0)
        vv = jnp.concatenate([bv, vn, pad], axis=0)
        kk_sw = pltpu.roll(kk, HEAD_DIM, 1)
        vv_sw = pltpu.roll(vv, HEAD_DIM, 1)
        q = q_ref[si]
        for kv in range(A_KV_HEADS):
            kpads = _head_pads(kk, kk_sw, kv == 0)
            vpads = _head_pads(vv, vv_sw, kv == 0)
            qs = jnp.concatenate(
                [q[:, (kv * pairs_per_kv + jj) * LANES:(kv * pairs_per_kv + jj + 1) * LANES]
                 for jj in range(pairs_per_kv)], axis=0).astype(BF16)
            o = jnp.zeros((rows, LANES), F32)
            for par in range(2):
                sink = jnp.zeros((rows, 1), F32)
                for jj in range(pairs_per_kv):
                    head = 2 * (kv * pairs_per_kv + jj) + par
                    sink = jnp.where(row_pair == jj, sink_ref[head], sink)
                sc = lax.dot_general(qs, kpads[par].astype(BF16), _NT, preferred_element_type=F32)
                sc = jnp.where(valid, sc, NEG_INF)
                m = jnp.maximum(jnp.max(sc, axis=-1, keepdims=True), sink)
                e = jnp.exp(sc - m)
                den = jnp.sum(e, axis=-1, keepdims=True) + jnp.exp(sink - m)
                o = o + jnp.dot((e * (1.0 / den)).astype(BF16), vpads[par].astype(BF16),
                                preferred_element_type=F32)
            for jj in range(pairs_per_kv):
                j = kv * pairs_per_kv + jj
                o_ref[si, :, j * LANES:(j + 1) * LANES] = o[jj * t:(jj + 1) * t, :]
        return carry

    lax.fori_loop(0, sb, body, 0)


def _swa_sample(q3, kn3, vn3, bk3, bv3, sinks, sb=16):
    b, t, dq = q3.shape
    wb, dk = bk3.shape[1], bk3.shape[2]

    def spec(rows, width):
        return pl.BlockSpec((sb, rows, width), lambda i: (i, 0, 0))

    return pl.pallas_call(
        _swa_sample_kernel,
        grid=(b // sb,),
        in_specs=[pl.BlockSpec(memory_space=pltpu.SMEM), spec(t, dq), spec(t, dk), spec(t, dk),
                  spec(wb, dk), spec(wb, dk)],
        out_specs=[spec(t, dq), spec(wb, dk), spec(wb, dk)],
        out_shape=[jax.ShapeDtypeStruct((b, t, dq), F32),
                   jax.ShapeDtypeStruct((b, wb, dk), F32),
                   jax.ShapeDtypeStruct((b, wb, dk), F32)],
        compiler_params=_cparams(("arbitrary",)),
        name="swa_sample",
    )(sinks, q3, kn3, vn3, bk3, bv3)


def _diff_lambda(lam_ref, layer):
    lf = lam_ref[...]
    s1 = jnp.sum(lf[0:1, :] * lf[1:2, :], axis=-1, keepdims=True)
    s2 = jnp.sum(lf[2:3, :] * lf[3:4, :], axis=-1, keepdims=True)
    return jnp.exp(s1) - jnp.exp(s2) + _lambda_init(layer)


def _diff_combine(acc0, l0, acc1, l1, lam, sub, layer):
    o = acc0 * (1.0 / l0) - lam * (acc1 * (1.0 / l1))
    ms = jnp.mean(o * o, axis=-1, keepdims=True)
    return (o * lax.rsqrt(ms + RMS_EPS)) * sub * (1.0 - _lambda_init(layer))


def _online_update(sc, v, m_ref, l_ref, acc_ref, idx):
    m_prev = m_ref[idx]
    m_new = jnp.maximum(m_prev, jnp.max(sc, axis=-1, keepdims=True))
    alpha = jnp.exp2(m_prev - m_new)
    e = jnp.exp2(sc - m_new)
    l_ref[idx] = alpha * l_ref[idx] + jnp.sum(e, axis=-1, keepdims=True)
    acc_ref[idx] = alpha * acc_ref[idx] + jnp.dot(e.astype(BF16), v, preferred_element_type=F32)
    m_ref[idx] = m_new


SUM_ROWS = 16


def _diff_prompt_kernel(qi_ref, ki_ref, q_ref, k_ref, v_ref, lam_ref, sub_ref, o_ref,
                        m_ref, acc_ref, *, layer):
    step = pl.program_id(1)
    qi, ki = qi_ref[step], ki_ref[step]
    tq = q_ref.shape[1]
    nh = B_KV_HEADS
    tk = k_ref.shape[1] // nh

    @pl.when(ki == 0)
    def _():
        m_ref[...] = jnp.full(m_ref.shape, NEG_INF, F32)
        acc_ref[...] = jnp.zeros(acc_ref.shape, F32)

    def update(masked):
        if masked:
            row = lax.broadcasted_iota(jnp.int32, (tk, 2 * tq), 0)
            col = lax.broadcasted_iota(jnp.int32, (tk, 2 * tq), 1) % tq
            causal = col >= row
        ones = jnp.ones((SUM_ROWS, tk), BF16)

        def scores(h):
            q = q_ref[0, :, h * 2 * LANES:(h + 1) * 2 * LANES]
            qs = jnp.concatenate([q[:, :LANES], q[:, LANES:]], axis=0)
            k = k_ref[0, pl.ds(h, tk, stride=nh), :]
            lane = lax.broadcasted_iota(jnp.int32, k.shape, 1)
            kcat = jnp.concatenate([jnp.where(lane < HEAD_DIM, k, 0.0),
                                    jnp.where(lane >= HEAD_DIM, k, 0.0)], axis=0).astype(BF16)
            return lax.dot_general(kcat, qs, _NT, preferred_element_type=F32)

        st_next = scores(0)
        for h in range(nh):
            st = st_next
            if h + 1 < nh:
                st_next = scores(h + 1)
            vt = jnp.concatenate([v_ref[0, pl.ds(h, tk, stride=nh), :].T.astype(BF16), ones], axis=0)
            for c in range(2):
                idx = 2 * h + c
                sc = st[c * tk:(c + 1) * tk, :]
                if masked:
                    sc = jnp.where(causal, sc, NEG_INF)
                m_prev = m_ref[idx]
                m_new = jnp.maximum(m_prev, jnp.max(sc, axis=0, keepdims=True))
                alpha = jnp.exp2(m_prev - m_new)
                e = jnp.exp2(sc - m_new).astype(BF16)
                acc_ref[idx] = alpha * acc_ref[idx] + jnp.dot(vt, e, preferred_element_type=F32)
                m_ref[idx] = m_new

    @pl.when(ki < qi)
    def _():
        update(False)

    @pl.when(ki == qi)
    def _():
        update(True)
        lam = _diff_lambda(lam_ref, layer)
        for h in range(nh):
            a0, a1 = acc_ref[2 * h], acc_ref[2 * h + 1]
            ot = (a0[:B_VDIM] * (1.0 / a0[B_VDIM:B_VDIM + 1])
                  - lam * (a1[:B_VDIM] * (1.0 / a1[B_VDIM:B_VDIM + 1])))
            o = ot.T
            ms = jnp.mean(o * o, axis=-1, keepdims=True)
            o = (o * lax.rsqrt(ms + RMS_EPS)) * sub_ref[...] * (1.0 - _lambda_init(layer))
            o_ref[0, :, (2 * h) * LANES:(2 * h + 1) * LANES] = o[:tq].astype(o_ref.dtype)
            o_ref[0, :, (2 * h + 1) * LANES:(2 * h + 2) * LANES] = o[tq:].astype(o_ref.dtype)


def _diff_prompt(q3, k_buf, v_buf, slot, lam_p, subln, layer, tile=512):
    b, s, dq = q3.shape
    nh = B_KV_HEADS
    tile = min(tile, s)
    nt = s // tile
    qi = jnp.asarray([i for i in range(nt) for _ in range(i + 1)], jnp.int32)
    ki = jnp.asarray([j for i in range(nt) for j in range(i + 1)], jnp.int32)
    kv_spec = pl.BlockSpec((None, 1, tile * nh, LANES), lambda bi, t, qr, kr: (slot, bi, kr[t], 0))
    grid_spec = pltpu.PrefetchScalarGridSpec(
        num_scalar_prefetch=2,
        grid=(b, int(qi.shape[0])),
        in_specs=[
            pl.BlockSpec((1, tile, dq), lambda bi, t, qr, kr: (bi, qr[t], 0)),
            kv_spec, kv_spec,
            pl.BlockSpec(lam_p.shape, lambda bi, t, qr, kr: (0, 0)),
            pl.BlockSpec((1, B_VDIM), lambda bi, t, qr, kr: (0, 0)),
        ],
        out_specs=pl.BlockSpec((1, tile, dq), lambda bi, t, qr, kr: (bi, qr[t], 0)),
        scratch_shapes=[pltpu.VMEM((2 * nh, 1, 2 * tile), F32),
                        pltpu.VMEM((2 * nh, B_VDIM + SUM_ROWS, 2 * tile), F32)],
    )
    return pl.pallas_call(
        functools.partial(_diff_prompt_kernel, layer=layer),
        grid_spec=grid_spec,
        out_shape=jax.ShapeDtypeStruct((b, s, dq), BF16),
        compiler_params=_cparams(("arbitrary", "arbitrary")),
        name="diff_prompt",
    )(qi, ki, q3, k_buf, v_buf, lam_p, subln.reshape(1, B_VDIM))


def _diff_sample_kernel(pt_ref, q_ref, kn_ref, vn_ref, lam_ref, sub_ref, *rest, pages, layer):
    k_refs, v_refs = rest[:pages], rest[pages:2 * pages]
    o_ref = rest[2 * pages]
    wt_ref, kb_ref, vb_ref, m_ref, l_ref, acc_ref = rest[2 * pages + 1:]
    step = pl.program_id(1)
    t = q_ref.shape[1]
    grp = 2 * t

    @pl.when(step == 0)
    def _():
        wt_ref[...] = jnp.zeros(wt_ref.shape, BF16)
        q = q_ref[0]
        lane = lax.broadcasted_iota(jnp.int32, (grp, LANES), 1)
        for kv in range(B_KV_HEADS):
            qq = jnp.concatenate([q[:, (2 * kv) * LANES:(2 * kv + 1) * LANES],
                                  q[:, (2 * kv + 1) * LANES:(2 * kv + 2) * LANES]], axis=0)
            r0 = kv * 2 * grp
            wt_ref[r0:r0 + grp, kv * LANES:(kv + 1) * LANES] = (
                jnp.where(lane < HEAD_DIM, qq, 0.0).astype(BF16))
            wt_ref[r0 + grp:r0 + 2 * grp, kv * LANES:(kv + 1) * LANES] = (
                jnp.where(lane >= HEAD_DIM, qq, 0.0).astype(BF16))
        m_ref[...] = jnp.full(m_ref.shape, NEG_INF, F32)
        l_ref[...] = jnp.zeros(l_ref.shape, F32)
        acc_ref[...] = jnp.zeros(acc_ref.shape, F32)

    for p in range(pages):
        for kv in range(B_KV_HEADS):
            rows = pl.ds(kv, PAGE_SIZE, stride=B_KV_HEADS)
            dst = (slice(p * PAGE_SIZE, (p + 1) * PAGE_SIZE), slice(kv * LANES, (kv + 1) * LANES))
            kb_ref[dst] = k_refs[p][rows, :].astype(BF16)
            vb_ref[dst] = v_refs[p][rows, :].astype(BF16)
    wt = wt_ref[...]
    sc = lax.dot_general(wt, kb_ref[...], _NT, preferred_element_type=F32)
    _online_update(sc, vb_ref[...], m_ref, l_ref, acc_ref, 0)

    @pl.when(step == pl.num_programs(1) - 1)
    def _():
        nrows = wt.shape[0]
        padz = jnp.zeros((LANES - t, kn_ref.shape[2]), F32)
        knb = jnp.concatenate([kn_ref[0], padz], axis=0).astype(BF16)
        vnb = jnp.concatenate([vn_ref[0], padz], axis=0).astype(BF16)
        sn = lax.dot_general(wt, knb, _NT, preferred_element_type=F32)
        row_t = lax.broadcasted_iota(jnp.int32, (nrows, LANES), 0) % t
        col = lax.broadcasted_iota(jnp.int32, (nrows, LANES), 1)
        sn = jnp.where((col < t) & (row_t >= col), sn, NEG_INF)
        _online_update(sn, vnb, m_ref, l_ref, acc_ref, 0)
        lam = _diff_lambda(lam_ref, layer)
        acc, l = acc_ref[0], l_ref[0]
        for kv in range(B_KV_HEADS):
            r0 = kv * 2 * grp
            ls = slice(kv * LANES, (kv + 1) * LANES)
            o = _diff_combine(acc[r0:r0 + grp, ls], l[r0:r0 + grp],
                              acc[r0 + grp:r0 + 2 * grp, ls], l[r0 + grp:r0 + 2 * grp],
                              lam, sub_ref[...], layer)
            o_ref[0, :, (2 * kv) * LANES:(2 * kv + 1) * LANES] = o[:t]
            o_ref[0, :, (2 * kv + 1) * LANES:(2 * kv + 2) * LANES] = o[t:]


def _diff_sample(q3, kn3, vn3, cache_k4, cache_v4, page_table, cache_layer, lam_p, subln,
                 layer, pages=32):
    b, t, dq = q3.shape
    n_pages = page_table.shape[1]
    pages = min(pages, n_pages)
    dk = B_KV_HEADS * cache_k4.shape[3]
    nrows = B_KV_HEADS * 2 * 2 * t
    pt_flat = page_table.reshape(-1)

    def page_spec(p):
        return pl.BlockSpec(
            (None, None, PAGE_SIZE * B_KV_HEADS, cache_k4.shape[3]),
            lambda bi, s, pt: (cache_layer, pt[bi * n_pages + s * pages + p], 0, 0))

    seq_spec = lambda width: pl.BlockSpec((1, t, width), lambda bi, s, pt: (bi, 0, 0))
    grid_spec = pltpu.PrefetchScalarGridSpec(
        num_scalar_prefetch=1,
        grid=(b, n_pages // pages),
        in_specs=[seq_spec(dq), seq_spec(dk), seq_spec(dk),
                  pl.BlockSpec(lam_p.shape, lambda bi, s, pt: (0, 0)),
                  pl.BlockSpec((1, B_VDIM), lambda bi, s, pt: (0, 0))]
                 + [page_spec(p) for p in range(pages)] + [page_spec(p) for p in range(pages)],
        out_specs=seq_spec(dq),
        scratch_shapes=[pltpu.VMEM((nrows, dk), BF16),
                        pltpu.VMEM((pages * PAGE_SIZE, dk), BF16),
                        pltpu.VMEM((pages * PAGE_SIZE, dk), BF16),
                        pltpu.VMEM((1, nrows, 1), F32), pltpu.VMEM((1, nrows, 1), F32),
                        pltpu.VMEM((1, nrows, dk), F32)],
    )
    return pl.pallas_call(
        functools.partial(_diff_sample_kernel, pages=pages, layer=layer),
        grid_spec=grid_spec,
        out_shape=jax.ShapeDtypeStruct((b, t, dq), F32),
        compiler_params=_cparams(("arbitrary", "arbitrary")),
        name="diff_sample",
    )(pt_flat, q3, kn3, vn3, lam_p, subln.reshape(1, B_VDIM),
      *([cache_k4] * pages), *([cache_v4] * pages))


def _trunk(x_prompt, x_sample, c_prompt, c_sample, state_win_k, state_win_v, cache_k, cache_v,
           page_table, ada_w, ada_b, norm1_w, norm2_w, mlp_w1, mlp_w2, a_wqkv, a_wo, a_sinks,
           b_wqkv, b_wo, b_lambda, b_subln, final_norm_w, past_len, prompt_rows, sample_seqs):
    bp, sp, d = x_prompt.shape
    bs, ts, _ = x_sample.shape
    depth = ada_w.shape[0]
    c_all = jnp.concatenate([c_sample, c_prompt], axis=0)
    mods = _modulation(c_all, ada_w, ada_b).reshape(depth, bs + bp, 1, N_MOD * d)
    tab_p = _rope_tables(jnp.arange(sp))
    tab_s = tuple(jnp.tile(tb, (sample_seqs, 1)) for tb in _rope_tables(past_len + jnp.arange(ts)))

    n_pool = cache_k.shape[1]
    n_b = depth // 2
    cache_k4 = cache_k.reshape(cache_k.shape[0], n_pool, PAGE_SIZE * B_KV_HEADS, -1)
    cache_v4 = cache_v.reshape(cache_v.shape[0], n_pool, PAGE_SIZE * B_KV_HEADS, -1)

    yp, ys = x_prompt, x_sample
    wkp, wvp, wks, wvs, kss, vss = ([] for _ in range(6))
    k_buf = v_buf = None
    for l in range(depth):
        i = l // 2
        mixer_a = l % 2 == 0
        wqkv = (a_wqkv[i] if mixer_a else b_wqkv[i]).astype(BF16)
        wo = (a_wo[i] if mixer_a else b_wo[i]).astype(BF16)
        w1, w2 = mlp_w1[l].astype(BF16), mlp_w2[l].astype(BF16)
        nq = d
        nk = (A_KV_HEADS * HEAD_DIM) if mixer_a else (B_KV_HEADS * 2 * HEAD_DIM)
        q_scale = SM_SCALE if mixer_a else SM_SCALE_LOG2
        kv_slot = None if mixer_a else (i, n_b, k_buf, v_buf)
        qp, kp, vp = _ln_qkv(yp, mods, bs, l, norm1_w[l], wqkv, tab_p, True,
                             1, prompt_rows, nq, nk, BF16, q_scale, kv_slot)
        qs, ks, vs = _ln_qkv(ys, mods, 0, l, norm1_w[l], wqkv, tab_s, False,
                             sample_seqs, ts, nq, nk, F32, q_scale)
        qp3 = qp.reshape(bp, sp, -1)
        qs3, ks3, vs3 = (a.reshape(bs, ts, -1) for a in (qs, ks, vs))
        if mixer_a:
            kp3, vp3 = kp.reshape(bp, sp, -1), vp.reshape(bp, sp, -1)
            op = _swa_prompt(qp3, kp3, vp3, a_sinks[i])
            wb = state_win_k.shape[2]
            os_, wk, wv = _swa_sample(qs3, ks3, vs3, state_win_k[i].reshape(bs, wb, -1),
                                      state_win_v[i].reshape(bs, wb, -1), a_sinks[i])
            keep = min(WINDOW, sp)
            wkp.append(kp3[:, sp - keep:].reshape(bp, keep, A_KV_HEADS, HEAD_DIM))
            wvp.append(vp3[:, sp - keep:].reshape(bp, keep, A_KV_HEADS, HEAD_DIM))
            wks.append(wk.reshape(bs, wb, A_KV_HEADS, HEAD_DIM))
            wvs.append(wv.reshape(bs, wb, A_KV_HEADS, HEAD_DIM))
        else:
            k_buf, v_buf = kp, vp
            op = _diff_prompt(qp3, k_buf.reshape(n_b, bp, sp * B_KV_HEADS, LANES),
                              v_buf.reshape(n_b, bp, sp * B_KV_HEADS, LANES), i,
                              b_lambda[i], b_subln[i], l)
            os_ = _diff_sample(qs3, ks3, vs3, cache_k4, cache_v4, page_table, i,
                               b_lambda[i], b_subln[i], l)
            kss.append(ks3.reshape(bs, ts, B_KV_HEADS, 2 * HEAD_DIM))
            vss.append(vs3.reshape(bs, ts, B_KV_HEADS, B_VDIM))
        fw = final_norm_w if l == depth - 1 else None
        yp = _post(yp, op.reshape(bp * sp, d), mods, bs, l, wo, norm2_w[l], w1, w2,
                   fw, 1, prompt_rows)
        ys = _post(ys, os_.reshape(bs * ts, d), mods, 0, l, wo, norm2_w[l], w1, w2,
                   fw, sample_seqs, ts)
    return (yp, ys, jnp.stack(wkp), jnp.stack(wvp), jnp.stack(wks), jnp.stack(wvs),
            k_buf.reshape(n_b, bp, sp, B_KV_HEADS, 2 * HEAD_DIM),
            v_buf.reshape(n_b, bp, sp, B_KV_HEADS, B_VDIM),
            jnp.stack(kss), jnp.stack(vss))


def kernel(x_prompt, x_sample, c_prompt, c_sample, state_win_k, state_win_v, cache_k, cache_v,
           page_table, ada_w, ada_b, norm1_w, norm2_w, mlp_w1, mlp_w2, a_wqkv, a_wo, a_sinks,
           b_wqkv, b_wo, b_lambda, b_subln, final_norm_w):
    past_len = page_table.shape[1] * PAGE_SIZE
    return _trunk(x_prompt, x_sample, c_prompt, c_sample, state_win_k, state_win_v, cache_k,
                  cache_v, page_table, ada_w, ada_b, norm1_w, norm2_w, mlp_w1, mlp_w2, a_wqkv,
                  a_wo, a_sinks, b_wqkv, b_wo, b_lambda, b_subln, final_norm_w,
                  past_len=past_len, prompt_rows=min(512, x_prompt.shape[1]),
                  sample_seqs=min(64, x_sample.shape[0]))
```

```python
import functools
import math

import jax
import jax.numpy as jnp
from jax import lax
from jax.experimental import pallas as pl
from jax.experimental.pallas import tpu as pltpu

F32 = jnp.float32
BF16 = jnp.bfloat16

D_MODEL = 1024
DEPTH = 4
HEAD_DIM = 64
ROT_HALF = HEAD_DIM // 8
ROPE_THETA = 500000.0
WINDOW = 128
A_HEADS = 16
A_KV_HEADS = 2
B_KV_HEADS = 4
B_VDIM = 128
D_FF = 4 * D_MODEL
RMS_EPS = 1e-5
N_MOD = 6
PAGE_SIZE = 128
SM_SCALE = HEAD_DIM ** -0.5
SM_SCALE_LOG2 = SM_SCALE * math.log2(math.e)
LANES = 128
VMEM_LIMIT = 56 * 1024 * 1024
NEG_INF = float("-inf")

_NT = (((1,), (1,)), ((), ()))


def _cparams(sem):
    return pltpu.CompilerParams(dimension_semantics=sem, vmem_limit_bytes=VMEM_LIMIT)


def _lambda_init(layer):
    return 0.8 - 0.6 * math.exp(-0.3 * layer)


def _mod_kernel(c_ref, w_ref, b_ref, o_ref):
    a = jax.nn.silu(c_ref[...]).astype(BF16)
    o_ref[...] = jnp.dot(a, w_ref[...].astype(BF16), preferred_element_type=F32) + b_ref[...]


def _modulation(c_all, ada_w, ada_b, tn=1024):
    depth, d, n = ada_w.shape
    rows = c_all.shape[0]
    return pl.pallas_call(
        _mod_kernel,
        grid=(depth, n // tn),
        in_specs=[
            pl.BlockSpec((rows, d), lambda l, j: (0, 0)),
            pl.BlockSpec((None, d, tn), lambda l, j: (l, 0, j)),
            pl.BlockSpec((None, 1, tn), lambda l, j: (l, 0, j)),
        ],
        out_specs=pl.BlockSpec((None, rows, tn), lambda l, j: (l, 0, j)),
        out_shape=jax.ShapeDtypeStruct((depth, rows, n), F32),
        compiler_params=_cparams(("arbitrary", "arbitrary")),
        name="modulation",
    )(c_all, ada_w, ada_b.reshape(depth, 1, n))


def _adaln(x, nw, shift, scale):
    ms = jnp.mean(x * x, axis=-1, keepdims=True)
    return (x * lax.rsqrt(ms + RMS_EPS)) * nw * (1.0 + scale) + shift


def _ln_qkv_kernel(x_ref, sh_ref, sc_ref, nw_ref, w_ref, cos_ref, sa_ref, sb_ref, *rest,
                   nq, nk, q_scale, interleave):
    q_ref, k_ref, v_ref = rest[-3:]
    x = x_ref[...]
    nb, r, d = x.shape
    h = _adaln(x, nw_ref[...], sh_ref[...], sc_ref[...])
    h2 = h.reshape(nb * r, d).astype(BF16)
    acc = jnp.dot(h2, w_ref[...], preferred_element_type=F32)
    cos, sa, sb = cos_ref[...], sa_ref[...], sb_ref[...]

    def rope(blk):
        return (blk * cos + pltpu.roll(blk, LANES - ROT_HALF, 1) * sa
                + pltpu.roll(blk, ROT_HALF, 1) * sb)

    rows = nb * r
    for j in range(nq // LANES):
        sl = slice(j * LANES, (j + 1) * LANES)
        q_ref[:, sl] = (rope(acc[:, sl]) * q_scale).astype(q_ref.dtype)
    hk = nk // LANES
    hv = (acc.shape[1] - nq - nk) // LANES
    for j in range(hk):
        kj = rope(acc[:, nq + j * LANES:nq + (j + 1) * LANES])
        if interleave:
            k_ref[pl.ds(j, rows, stride=hk), :] = kj
        else:
            k_ref[:, j * LANES:(j + 1) * LANES] = kj
    if interleave:
        for j in range(hv):
            v_ref[pl.ds(j, rows, stride=hv), :] = acc[:, nq + nk + j * LANES:nq + nk + (j + 1) * LANES]
    else:
        v_ref[...] = acc[:, nq + nk:]


def _rope_tables(pos):
    inv = ROPE_THETA ** (-jnp.arange(ROT_HALF, dtype=F32) / ROT_HALF)
    ang = pos.astype(F32)[:, None] * inv[None, :]
    cos, sin = jnp.cos(ang), jnp.sin(ang)
    dd = jnp.arange(LANES) % HEAD_DIM
    idx = dd % ROT_HALF
    cos_l = jnp.where(dd < 2 * ROT_HALF, cos[:, idx], 1.0)
    sa = jnp.where(dd < ROT_HALF, -sin[:, idx], 0.0)
    sb = jnp.where((dd >= ROT_HALF) & (dd < 2 * ROT_HALF), sin[:, idx], 0.0)
    return cos_l, sa, sb


def _ln_qkv(x3, mods, mod_row0, layer, nw, w, tables, table_tiled, nb, r, nq, nk, q_dtype,
            q_scale, kv_slot=None):
    g, s, d = x3.shape
    n = w.shape[1]
    nv = n - nq - nk
    rows = nb * r
    gs, ss = g // nb, s // r
    t = g * s
    if table_tiled:
        tab_spec = pl.BlockSpec((rows, LANES), lambda i, j: (j, 0))
    else:
        tab_spec = pl.BlockSpec((rows, LANES), lambda i, j: (0, 0))

    def mod_spec(chunk):
        return pl.BlockSpec((None, nb, 1, d), lambda i, j: (layer, mod_row0 // nb + i, 0, chunk))

    def out_spec(width):
        return pl.BlockSpec((rows, width), lambda i, j: (i * ss + j, 0))

    in_specs = [
        pl.BlockSpec((nb, r, d), lambda i, j: (i, j, 0)),
        mod_spec(0), mod_spec(1),
        pl.BlockSpec((1, d), lambda i, j: (0, 0)),
        pl.BlockSpec((d, n), lambda i, j: (0, 0)),
        tab_spec, tab_spec, tab_spec,
    ]
    args = [x3, mods, mods, nw.reshape(1, d), w, *tables]
    aliases = {}
    if kv_slot is None:
        kv_specs = [out_spec(nk), out_spec(nv)]
        kv_shapes = [jax.ShapeDtypeStruct((t, nk), F32), jax.ShapeDtypeStruct((t, nv), F32)]
    else:
        slot, n_slots, k_buf, v_buf = kv_slot
        hk, hv = nk // LANES, nv // LANES

        def slot_spec(heads):
            return pl.BlockSpec((None, rows * heads, LANES), lambda i, j: (slot, i * ss + j, 0))

        kv_specs = [slot_spec(hk), slot_spec(hv)]
        kv_shapes = [jax.ShapeDtypeStruct((n_slots, t * hk, LANES), F32),
                     jax.ShapeDtypeStruct((n_slots, t * hv, LANES), F32)]
        if k_buf is not None:
            aliases = {len(args): 1, len(args) + 1: 2}
            in_specs += [pl.BlockSpec(memory_space=pl.ANY)] * 2
            args += [k_buf, v_buf]
    return pl.pallas_call(
        functools.partial(_ln_qkv_kernel, nq=nq, nk=nk, q_scale=q_scale,
                          interleave=kv_slot is not None),
        grid=(gs, ss),
        in_specs=in_specs,
        out_specs=[out_spec(nq)] + kv_specs,
        out_shape=[jax.ShapeDtypeStruct((t, nq), q_dtype)] + kv_shapes,
        input_output_aliases=aliases,
        compiler_params=_cparams(("arbitrary", "arbitrary")),
        name="ln_qkv_rope",
    )(*args)


def _post_kernel(*refs, final):
    if final:
        (y_ref, o_ref, g1_ref, sh_ref, sc_ref, g2_ref, wo_ref, nw_ref, w1_ref, w2_ref,
         fw_ref, out_ref) = refs
    else:
        (y_ref, o_ref, g1_ref, sh_ref, sc_ref, g2_ref, wo_ref, nw_ref, w1_ref, w2_ref,
         out_ref) = refs
    y = y_ref[...]
    nb, r, d = y.shape
    attn = jnp.dot(o_ref[...].astype(BF16), wo_ref[...], preferred_element_type=F32)
    y1 = y + g1_ref[...] * attn.reshape(nb, r, d)
    h2 = _adaln(y1, nw_ref[...], sh_ref[...], sc_ref[...]).reshape(nb * r, d).astype(BF16)
    t = jnp.dot(h2, w1_ref[...], preferred_element_type=F32)
    t = jnp.square(jnp.maximum(t, 0.0)).astype(BF16)
    mlp = jnp.dot(t, w2_ref[...], preferred_element_type=F32)
    y2 = y1 + g2_ref[...] * mlp.reshape(nb, r, d)
    if final:
        ms = jnp.mean(y2 * y2, axis=-1, keepdims=True)
        y2 = (y2 * lax.rsqrt(ms + RMS_EPS)) * fw_ref[...]
    out_ref[...] = y2


def _post(y3, o2, mods, mod_row0, layer, wo, nw, w1, w2, final_w, nb, r):
    g, s, d = y3.shape
    ff = w1.shape[1]
    rows = nb * r
    gs, ss = g // nb, s // r
    final = final_w is not None

    def mod_spec(chunk):
        return pl.BlockSpec((None, nb, 1, d), lambda i, j: (layer, mod_row0 // nb + i, 0, chunk))

    def const_spec(shape):
        return pl.BlockSpec(shape, lambda i, j: (0, 0), pipeline_mode=pl.Buffered(1))

    in_specs = [
        pl.BlockSpec((nb, r, d), lambda i, j: (i, j, 0)),
        pl.BlockSpec((rows, d), lambda i, j: (i * ss + j, 0)),
        mod_spec(2), mod_spec(3), mod_spec(4), mod_spec(5),
        const_spec((d, d)), const_spec((1, d)), const_spec((d, ff)), const_spec((ff, d)),
    ]
    args = [y3, o2, mods, mods, mods, mods, wo, nw.reshape(1, d), w1, w2]
    if final:
        in_specs.append(const_spec((1, d)))
        args.append(final_w.reshape(1, d))
    return pl.pallas_call(
        functools.partial(_post_kernel, final=final),
        grid=(gs, ss),
        in_specs=in_specs,
        out_specs=pl.BlockSpec((nb, r, d), lambda i, j: (i, j, 0)),
        out_shape=jax.ShapeDtypeStruct((g, s, d), F32),
        compiler_params=_cparams(("arbitrary", "arbitrary")),
        name="proj_mlp",
    )(*args)


def _head_pads(x, x_sw, first):
    lane = lax.broadcasted_iota(jnp.int32, x.shape, 1)
    lo, hi = lane < HEAD_DIM, lane >= HEAD_DIM
    if first:
        return jnp.where(lo, x, 0.0), jnp.where(hi, x_sw, 0.0)
    return jnp.where(lo, x_sw, 0.0), jnp.where(hi, x, 0.0)


def _swa_prompt_kernel(sink_ref, q_ref, kp_ref, kc_ref, vp_ref, vc_ref, o_ref):
    i = pl.program_id(1)
    w = WINDOW
    kk = jnp.concatenate([kp_ref[0], kc_ref[0]], axis=0)
    vv = jnp.concatenate([vp_ref[0], vc_ref[0]], axis=0)
    kk_sw = pltpu.roll(kk, HEAD_DIM, 1)
    vv_sw = pltpu.roll(vv, HEAD_DIM, 1)
    row = lax.broadcasted_iota(jnp.int32, (w, 2 * w), 0)
    col = lax.broadcasted_iota(jnp.int32, (w, 2 * w), 1)
    valid = (col > row) & (col <= row + w) & ((col >= w) | (i > 0))
    pairs_per_kv = A_HEADS // A_KV_HEADS // 2
    npairs = A_HEADS // 2
    halves = range(2)
    kcats, vcats = [], []
    for kv in range(A_KV_HEADS):
        k0, k1 = _head_pads(kk, kk_sw, kv == 0)
        v0, v1 = _head_pads(vv, vv_sw, kv == 0)
        kcats.append(jnp.concatenate([k0, k1], axis=0).astype(BF16))
        vcats.append(jnp.concatenate([v0, v1], axis=0).astype(BF16))
    ss = [lax.dot_general(q_ref[0, :, j * LANES:(j + 1) * LANES], kcats[j // pairs_per_kv], _NT,
                          preferred_element_type=F32) for j in range(npairs)]
    scs = [[jnp.where(valid, ss[j][:, par * 2 * w:(par + 1) * 2 * w], NEG_INF) for par in halves]
           for j in range(npairs)]
    ms = [[jnp.maximum(jnp.max(scs[j][par], axis=-1, keepdims=True), sink_ref[2 * j + par])
           for par in halves] for j in range(npairs)]
    es = [[jnp.exp(scs[j][par] - ms[j][par]) for par in halves] for j in range(npairs)]
    dens = [[jnp.sum(es[j][par], axis=-1, keepdims=True)
             + jnp.exp(sink_ref[2 * j + par] - ms[j][par]) for par in halves]
            for j in range(npairs)]
    ps = [jnp.concatenate([(es[j][par] * (1.0 / dens[j][par])).astype(BF16) for par in halves],
                          axis=1) for j in range(npairs)]
    for j in range(npairs):
        o = jnp.dot(ps[j], vcats[j // pairs_per_kv], preferred_element_type=F32)
        o_ref[0, :, j * LANES:(j + 1) * LANES] = o.astype(o_ref.dtype)


def _swa_prompt(q3, k3, v3, sinks):
    b, s, dq = q3.shape
    nb = s // WINDOW
    dk = k3.shape[2]
    q_spec = pl.BlockSpec((1, WINDOW, dq), lambda bi, i: (bi, i, 0))
    prev = pl.BlockSpec((1, WINDOW, dk), lambda bi, i: (bi, jnp.maximum(i - 1, 0), 0))
    cur = pl.BlockSpec((1, WINDOW, dk), lambda bi, i: (bi, i, 0))
    return pl.pallas_call(
        _swa_prompt_kernel,
        grid=(b, nb),
        in_specs=[pl.BlockSpec(memory_space=pltpu.SMEM), q_spec, prev, cur, prev, cur],
        out_specs=q_spec,
        out_shape=jax.ShapeDtypeStruct((b, s, dq), BF16),
        compiler_params=_cparams(("arbitrary", "arbitrary")),
        name="swa_prompt",
    )(sinks, q3, k3, k3, v3, v3)


def _swa_sample_kernel(sink_ref, q_ref, kn_ref, vn_ref, bk_ref, bv_ref, o_ref, wk_ref, wv_ref):
    sb, t, _ = q_ref.shape
    wb = bk_ref.shape[1]
    nkeys = 2 * wb
    pairs_per_kv = A_HEADS // A_KV_HEADS // 2
    rows = pairs_per_kv * t
    row_t = lax.broadcasted_iota(jnp.int32, (rows, nkeys), 0) % t
    col = lax.broadcasted_iota(jnp.int32, (rows, nkeys), 1)
    valid = (col > row_t + wb - WINDOW) & (col <= row_t + wb)
    row_pair = lax.broadcasted_iota(jnp.int32, (rows, 1), 0) // t
    pad = jnp.zeros((wb - t, LANES), F32)
    combos = [(kv, par) for kv in range(A_KV_HEADS) for par in range(2)]
    sinks = []
    for kv, par in combos:
        sink = jnp.zeros((rows, 1), F32)
        for jj in range(pairs_per_kv):
            sink = jnp.where(row_pair == jj, sink_ref[2 * (kv * pairs_per_kv + jj) + par], sink)
        sinks.append(sink)

    def body(si, carry):
        kn, vn, bk, bv = kn_ref[si], vn_ref[si], bk_ref[si], bv_ref[si]
        wk_ref[si, 0:wb - t, :] = bk[t:, :]
        wk_ref[si, wb - t:wb, :] = kn
        wv_ref[si, 0:wb - t, :] = bv[t:, :]
        wv_ref[si, wb - t:wb, :] = vn
        kk = jnp.concatenate([bk, kn, pad], axis=0)
        vv = jnp.concatenate([bv, vn, pad], axis=0)
        kk_sw = pltpu.roll(kk, HEAD_DIM, 1)
        vv_sw = pltpu.roll(vv, HEAD_DIM, 1)
        q = q_ref[si]
        kpads = [_head_pads(kk, kk_sw, kv == 0) for kv in range(A_KV_HEADS)]
        vpads = [_head_pads(vv, vv_sw, kv == 0) for kv in range(A_KV_HEADS)]
        qss = [jnp.concatenate(
            [q[:, (kv * pairs_per_kv + jj) * LANES:(kv * pairs_per_kv + jj + 1) * LANES]
             for jj in range(pairs_per_kv)], axis=0).astype(BF16) for kv in range(A_KV_HEADS)]
        scs = [jnp.where(valid, lax.dot_general(qss[kv], kpads[kv][par].astype(BF16), _NT,
                                                preferred_element_type=F32), NEG_INF)
               for kv, par in combos]
        ms = [jnp.maximum(jnp.max(scs[c], axis=-1, keepdims=True), sinks[c])
              for c in range(len(combos))]
        es = [jnp.exp(scs[c] - ms[c]) for c in range(len(combos))]
        dens = [jnp.sum(es[c], axis=-1, keepdims=True) + jnp.exp(sinks[c] - ms[c])
                for c in range(len(combos))]
        outs = [jnp.dot((es[c] * (1.0 / dens[c])).astype(BF16), vpads[kv][par].astype(BF16),
                        preferred_element_type=F32) for c, (kv, par) in enumerate(combos)]
        for kv in range(A_KV_HEADS):
            o = outs[2 * kv] + outs[2 * kv + 1]
            for jj in range(pairs_per_kv):
                j = kv * pairs_per_kv + jj
                o_ref[si, :, j * LANES:(j + 1) * LANES] = o[jj * t:(jj + 1) * t, :]
        return carry

    lax.fori_loop(0, sb, body, 0)


def _swa_sample(q3, kn3, vn3, bk3, bv3, sinks, sb=16):
    b, t, dq = q3.shape
    wb, dk = bk3.shape[1], bk3.shape[2]

    def spec(rows, width):
        return pl.BlockSpec((sb, rows, width), lambda i: (i, 0, 0))

    return pl.pallas_call(
        _swa_sample_kernel,
        grid=(b // sb,),
        in_specs=[pl.BlockSpec(memory_space=pltpu.SMEM), spec(t, dq), spec(t, dk), spec(t, dk),
                  spec(wb, dk), spec(wb, dk)],
        out_specs=[spec(t, dq), spec(wb, dk), spec(wb, dk)],
        out_shape=[jax.ShapeDtypeStruct((b, t, dq), F32),
                   jax.ShapeDtypeStruct((b, wb, dk), F32),
                   jax.ShapeDtypeStruct((b, wb, dk), F32)],
        compiler_params=_cparams(("arbitrary",)),
        name="swa_sample",
    )(sinks, q3, kn3, vn3, bk3, bv3)


def _diff_lambda(lam_ref, layer):
    lf = lam_ref[...]
    s1 = jnp.sum(lf[0:1, :] * lf[1:2, :], axis=-1, keepdims=True)
    s2 = jnp.sum(lf[2:3, :] * lf[3:4, :], axis=-1, keepdims=True)
    return jnp.exp(s1) - jnp.exp(s2) + _lambda_init(layer)


def _diff_combine(acc0, l0, acc1, l1, lam, sub, layer):
    o = acc0 * (1.0 / l0) - lam * (acc1 * (1.0 / l1))
    ms = jnp.mean(o * o, axis=-1, keepdims=True)
    return (o * lax.rsqrt(ms + RMS_EPS)) * sub * (1.0 - _lambda_init(layer))


def _online_update(sc, v, m_ref, l_ref, acc_ref, idx):
    m_prev = m_ref[idx]
    m_new = jnp.maximum(m_prev, jnp.max(sc, axis=-1, keepdims=True))
    alpha = jnp.exp2(m_prev - m_new)
    e = jnp.exp2(sc - m_new)
    l_ref[idx] = alpha * l_ref[idx] + jnp.sum(e, axis=-1, keepdims=True)
    acc_ref[idx] = alpha * acc_ref[idx] + jnp.dot(e.astype(BF16), v, preferred_element_type=F32)
    m_ref[idx] = m_new


SUM_ROWS = 16


def _diff_prompt_kernel(qi_ref, ki_ref, q_ref, k_ref, v_ref, lam_ref, sub_ref, o_ref,
                        m_ref, acc_ref, *, layer):
    step = pl.program_id(1)
    qi, ki = qi_ref[step], ki_ref[step]
    tq = q_ref.shape[1]
    nh = B_KV_HEADS
    tk = k_ref.shape[1] // nh

    @pl.when(ki == 0)
    def _():
        m_ref[...] = jnp.full(m_ref.shape, NEG_INF, F32)
        acc_ref[...] = jnp.zeros(acc_ref.shape, F32)

    def update(masked):
        if masked:
            row = lax.broadcasted_iota(jnp.int32, (tk, 2 * tq), 0)
            col = lax.broadcasted_iota(jnp.int32, (tk, 2 * tq), 1) % tq
            causal = col >= row
        ones = jnp.ones((SUM_ROWS, tk), BF16)

        def scores(h):
            q = q_ref[0, :, h * 2 * LANES:(h + 1) * 2 * LANES]
            qs = jnp.concatenate([q[:, :LANES], q[:, LANES:]], axis=0)
            k = k_ref[0, pl.ds(h, tk, stride=nh), :]
            lane = lax.broadcasted_iota(jnp.int32, k.shape, 1)
            kcat = jnp.concatenate([jnp.where(lane < HEAD_DIM, k, 0.0),
                                    jnp.where(lane >= HEAD_DIM, k, 0.0)], axis=0).astype(BF16)
            return lax.dot_general(kcat, qs, _NT, preferred_element_type=F32)

        st_next = scores(0)
        for h in range(nh):
            st = st_next
            if h + 1 < nh:
                st_next = scores(h + 1)
            vt = jnp.concatenate([v_ref[0, pl.ds(h, tk, stride=nh), :].T.astype(BF16), ones], axis=0)
            for c in range(2):
                idx = 2 * h + c
                sc = st[c * tk:(c + 1) * tk, :]
                if masked:
                    sc = jnp.where(causal, sc, NEG_INF)
                m_prev = m_ref[idx]
                m_new = jnp.maximum(m_prev, jnp.max(sc, axis=0, keepdims=True))
                alpha = jnp.exp2(m_prev - m_new)
                e = jnp.exp2(sc - m_new).astype(BF16)
                acc_ref[idx] = alpha * acc_ref[idx] + jnp.dot(vt, e, preferred_element_type=F32)
                m_ref[idx] = m_new

    @pl.when(ki < qi)
    def _():
        update(False)

    @pl.when(ki == qi)
    def _():
        update(True)
        lam = _diff_lambda(lam_ref, layer)
        for h in range(nh):
            a0, a1 = acc_ref[2 * h], acc_ref[2 * h + 1]
            ot = (a0[:B_VDIM] * (1.0 / a0[B_VDIM:B_VDIM + 1])
                  - lam * (a1[:B_VDIM] * (1.0 / a1[B_VDIM:B_VDIM + 1])))
            o = ot.T
            ms = jnp.mean(o * o, axis=-1, keepdims=True)
            o = (o * lax.rsqrt(ms + RMS_EPS)) * sub_ref[...] * (1.0 - _lambda_init(layer))
            o_ref[0, :, (2 * h) * LANES:(2 * h + 1) * LANES] = o[:tq].astype(o_ref.dtype)
            o_ref[0, :, (2 * h + 1) * LANES:(2 * h + 2) * LANES] = o[tq:].astype(o_ref.dtype)


def _diff_prompt(q3, k_buf, v_buf, slot, lam_p, subln, layer, tile=512):
    b, s, dq = q3.shape
    nh = B_KV_HEADS
    tile = min(tile, s)
    nt = s // tile
    qi = jnp.asarray([i for i in range(nt) for _ in range(i + 1)], jnp.int32)
    ki = jnp.asarray([j for i in range(nt) for j in range(i + 1)], jnp.int32)
    kv_spec = pl.BlockSpec((None, 1, tile * nh, LANES), lambda bi, t, qr, kr: (slot, bi, kr[t], 0))
    grid_spec = pltpu.PrefetchScalarGridSpec(
        num_scalar_prefetch=2,
        grid=(b, int(qi.shape[0])),
        in_specs=[
            pl.BlockSpec((1, tile, dq), lambda bi, t, qr, kr: (bi, qr[t], 0)),
            kv_spec, kv_spec,
            pl.BlockSpec(lam_p.shape, lambda bi, t, qr, kr: (0, 0)),
            pl.BlockSpec((1, B_VDIM), lambda bi, t, qr, kr: (0, 0)),
        ],
        out_specs=pl.BlockSpec((1, tile, dq), lambda bi, t, qr, kr: (bi, qr[t], 0)),
        scratch_shapes=[pltpu.VMEM((2 * nh, 1, 2 * tile), F32),
                        pltpu.VMEM((2 * nh, B_VDIM + SUM_ROWS, 2 * tile), F32)],
    )
    return pl.pallas_call(
        functools.partial(_diff_prompt_kernel, layer=layer),
        grid_spec=grid_spec,
        out_shape=jax.ShapeDtypeStruct((b, s, dq), BF16),
        compiler_params=_cparams(("arbitrary", "arbitrary")),
        name="diff_prompt",
    )(qi, ki, q3, k_buf, v_buf, lam_p, subln.reshape(1, B_VDIM))


def _diff_sample_kernel(pt_ref, q_ref, kn_ref, vn_ref, lam_ref, sub_ref, *rest, pages, layer):
    k_refs, v_refs = rest[:pages], rest[pages:2 * pages]
    o_ref = rest[2 * pages]
    wt_ref, kb_ref, vb_ref, m_ref, l_ref, acc_ref = rest[2 * pages + 1:]
    step = pl.program_id(1)
    t = q_ref.shape[1]
    grp = 2 * t

    @pl.when(step == 0)
    def _():
        wt_ref[...] = jnp.zeros(wt_ref.shape, BF16)
        q = q_ref[0]
        lane = lax.broadcasted_iota(jnp.int32, (grp, LANES), 1)
        for kv in range(B_KV_HEADS):
            qq = jnp.concatenate([q[:, (2 * kv) * LANES:(2 * kv + 1) * LANES],
                                  q[:, (2 * kv + 1) * LANES:(2 * kv + 2) * LANES]], axis=0)
            r0 = kv * 2 * grp
            wt_ref[r0:r0 + grp, kv * LANES:(kv + 1) * LANES] = (
                jnp.where(lane < HEAD_DIM, qq, 0.0).astype(BF16))
            wt_ref[r0 + grp:r0 + 2 * grp, kv * LANES:(kv + 1) * LANES] = (
                jnp.where(lane >= HEAD_DIM, qq, 0.0).astype(BF16))
        m_ref[...] = jnp.full(m_ref.shape, NEG_INF, F32)
        l_ref[...] = jnp.zeros(l_ref.shape, F32)
        acc_ref[...] = jnp.zeros(acc_ref.shape, F32)

    for p in range(pages):
        for kv in range(B_KV_HEADS):
            rows = pl.ds(kv, PAGE_SIZE, stride=B_KV_HEADS)
            dst = (slice(p * PAGE_SIZE, (p + 1) * PAGE_SIZE), slice(kv * LANES, (kv + 1) * LANES))
            kb_ref[dst] = k_refs[p][rows, :].astype(BF16)
            vb_ref[dst] = v_refs[p][rows, :].astype(BF16)
    wt = wt_ref[...]
    chains = m_ref.shape[0]
    per = pages * PAGE_SIZE // chains
    scs = [lax.dot_general(wt, kb_ref[ch * per:(ch + 1) * per, :], _NT, preferred_element_type=F32)
           for ch in range(chains)]
    for ch in range(chains):
        _online_update(scs[ch], vb_ref[ch * per:(ch + 1) * per, :], m_ref, l_ref, acc_ref, ch)

    @pl.when(step == pl.num_programs(1) - 1)
    def _():
        nrows = wt.shape[0]
        padz = jnp.zeros((LANES - t, kn_ref.shape[2]), F32)
        knb = jnp.concatenate([kn_ref[0], padz], axis=0).astype(BF16)
        vnb = jnp.concatenate([vn_ref[0], padz], axis=0).astype(BF16)
        sn = lax.dot_general(wt, knb, _NT, preferred_element_type=F32)
        row_t = lax.broadcasted_iota(jnp.int32, (nrows, LANES), 0) % t
        col = lax.broadcasted_iota(jnp.int32, (nrows, LANES), 1)
        sn = jnp.where((col < t) & (row_t >= col), sn, NEG_INF)
        _online_update(sn, vnb, m_ref, l_ref, acc_ref, 0)
        m_all = m_ref[0]
        for ch in range(1, chains):
            m_all = jnp.maximum(m_all, m_ref[ch])
        l = jnp.zeros(l_ref.shape[1:], F32)
        acc = jnp.zeros(acc_ref.shape[1:], F32)
        for ch in range(chains):
            wgt = jnp.exp2(m_ref[ch] - m_all)
            l = l + wgt * l_ref[ch]
            acc = acc + wgt * acc_ref[ch]
        lam = _diff_lambda(lam_ref, layer)
        for kv in range(B_KV_HEADS):
            r0 = kv * 2 * grp
            ls = slice(kv * LANES, (kv + 1) * LANES)
            o = _diff_combine(acc[r0:r0 + grp, ls], l[r0:r0 + grp],
                              acc[r0 + grp:r0 + 2 * grp, ls], l[r0 + grp:r0 + 2 * grp],
                              lam, sub_ref[...], layer)
            o_ref[0, :, (2 * kv) * LANES:(2 * kv + 1) * LANES] = o[:t]
            o_ref[0, :, (2 * kv + 1) * LANES:(2 * kv + 2) * LANES] = o[t:]


def _diff_sample(q3, kn3, vn3, cache_k4, cache_v4, page_table, cache_layer, lam_p, subln,
                 layer, pages=32, chains=2):
    b, t, dq = q3.shape
    n_pages = page_table.shape[1]
    pages = min(pages, n_pages)
    chains = min(chains, pages)
    dk = B_KV_HEADS * cache_k4.shape[3]
    nrows = B_KV_HEADS * 2 * 2 * t
    pt_flat = page_table.reshape(-1)

    def page_spec(p):
        return pl.BlockSpec(
            (None, None, PAGE_SIZE * B_KV_HEADS, cache_k4.shape[3]),
            lambda bi, s, pt: (cache_layer, pt[bi * n_pages + s * pages + p], 0, 0))

    seq_spec = lambda width: pl.BlockSpec((1, t, width), lambda bi, s, pt: (bi, 0, 0))
    grid_spec = pltpu.PrefetchScalarGridSpec(
        num_scalar_prefetch=1,
        grid=(b, n_pages // pages),
        in_specs=[seq_spec(dq), seq_spec(dk), seq_spec(dk),
                  pl.BlockSpec(lam_p.shape, lambda bi, s, pt: (0, 0)),
                  pl.BlockSpec((1, B_VDIM), lambda bi, s, pt: (0, 0))]
                 + [page_spec(p) for p in range(pages)] + [page_spec(p) for p in range(pages)],
        out_specs=seq_spec(dq),
        scratch_shapes=[pltpu.VMEM((nrows, dk), BF16),
                        pltpu.VMEM((pages * PAGE_SIZE, dk), BF16),
                        pltpu.VMEM((pages * PAGE_SIZE, dk), BF16),
                        pltpu.VMEM((chains, nrows, 1), F32), pltpu.VMEM((chains, nrows, 1), F32),
                        pltpu.VMEM((chains, nrows, dk), F32)],
    )
    return pl.pallas_call(
        functools.partial(_diff_sample_kernel, pages=pages, layer=layer),
        grid_spec=grid_spec,
        out_shape=jax.ShapeDtypeStruct((b, t, dq), F32),
        compiler_params=_cparams(("arbitrary", "arbitrary")),
        name="diff_sample",
    )(pt_flat, q3, kn3, vn3, lam_p, subln.reshape(1, B_VDIM),
      *([cache_k4] * pages), *([cache_v4] * pages))


def _trunk(x_prompt, x_sample, c_prompt, c_sample, state_win_k, state_win_v, cache_k, cache_v,
           page_table, ada_w, ada_b, norm1_w, norm2_w, mlp_w1, mlp_w2, a_wqkv, a_wo, a_sinks,
           b_wqkv, b_wo, b_lambda, b_subln, final_norm_w, past_len, prompt_rows, sample_seqs):
    bp, sp, d = x_prompt.shape
    bs, ts, _ = x_sample.shape
    depth = ada_w.shape[0]
    c_all = jnp.concatenate([c_sample, c_prompt], axis=0)
    mods = _modulation(c_all, ada_w, ada_b).reshape(depth, bs + bp, 1, N_MOD * d)
    tab_p = _rope_tables(jnp.arange(sp))
    tab_s = tuple(jnp.tile(tb, (sample_seqs, 1)) for tb in _rope_tables(past_len + jnp.arange(ts)))

    n_pool = cache_k.shape[1]
    n_b = depth // 2
    cache_k4 = cache_k.reshape(cache_k.shape[0], n_pool, PAGE_SIZE * B_KV_HEADS, -1)
    cache_v4 = cache_v.reshape(cache_v.shape[0], n_pool, PAGE_SIZE * B_KV_HEADS, -1)

    yp, ys = x_prompt, x_sample
    wkp, wvp, wks, wvs, kss, vss = ([] for _ in range(6))
    k_buf = v_buf = None
    for l in range(depth):
        i = l // 2
        mixer_a = l % 2 == 0
        wqkv = (a_wqkv[i] if mixer_a else b_wqkv[i]).astype(BF16)
        wo = (a_wo[i] if mixer_a else b_wo[i]).astype(BF16)
        w1, w2 = mlp_w1[l].astype(BF16), mlp_w2[l].astype(BF16)
        nq = d
        nk = (A_KV_HEADS * HEAD_DIM) if mixer_a else (B_KV_HEADS * 2 * HEAD_DIM)
        q_scale = SM_SCALE if mixer_a else SM_SCALE_LOG2
        kv_slot = None if mixer_a else (i, n_b, k_buf, v_buf)
        qp, kp, vp = _ln_qkv(yp, mods, bs, l, norm1_w[l], wqkv, tab_p, True,
                             1, prompt_rows, nq, nk, BF16, q_scale, kv_slot)
        qs, ks, vs = _ln_qkv(ys, mods, 0, l, norm1_w[l], wqkv, tab_s, False,
                             sample_seqs, ts, nq, nk, F32, q_scale)
        qp3 = qp.reshape(bp, sp, -1)
        qs3, ks3, vs3 = (a.reshape(bs, ts, -1) for a in (qs, ks, vs))
        if mixer_a:
            kp3, vp3 = kp.reshape(bp, sp, -1), vp.reshape(bp, sp, -1)
            op = _swa_prompt(qp3, kp3, vp3, a_sinks[i])
            wb = state_win_k.shape[2]
            os_, wk, wv = _swa_sample(qs3, ks3, vs3, state_win_k[i].reshape(bs, wb, -1),
                                      state_win_v[i].reshape(bs, wb, -1), a_sinks[i])
            keep = min(WINDOW, sp)
            wkp.append(kp3[:, sp - keep:].reshape(bp, keep, A_KV_HEADS, HEAD_DIM))
            wvp.append(vp3[:, sp - keep:].reshape(bp, keep, A_KV_HEADS, HEAD_DIM))
            wks.append(wk.reshape(bs, wb, A_KV_HEADS, HEAD_DIM))
            wvs.append(wv.reshape(bs, wb, A_KV_HEADS, HEAD_DIM))
        else:
            k_buf, v_buf = kp, vp
            op = _diff_prompt(qp3, k_buf.reshape(n_b, bp, sp * B_KV_HEADS, LANES),
                              v_buf.reshape(n_b, bp, sp * B_KV_HEADS, LANES), i,
                              b_lambda[i], b_subln[i], l)
            os_ = _diff_sample(qs3, ks3, vs3, cache_k4, cache_v4, page_table, i,
                               b_lambda[i], b_subln[i], l)
            kss.append(ks3.reshape(bs, ts, B_KV_HEADS, 2 * HEAD_DIM))
            vss.append(vs3.reshape(bs, ts, B_KV_HEADS, B_VDIM))
        fw = final_norm_w if l == depth - 1 else None
        yp = _post(yp, op.reshape(bp * sp, d), mods, bs, l, wo, norm2_w[l], w1, w2,
                   fw, 1, prompt_rows)
        ys = _post(ys, os_.reshape(bs * ts, d), mods, 0, l, wo, norm2_w[l], w1, w2,
                   fw, sample_seqs, ts)
    return (yp, ys, jnp.stack(wkp), jnp.stack(wvp), jnp.stack(wks), jnp.stack(wvs),
            k_buf.reshape(n_b, bp, sp, B_KV_HEADS, 2 * HEAD_DIM),
            v_buf.reshape(n_b, bp, sp, B_KV_HEADS, B_VDIM),
            jnp.stack(kss), jnp.stack(vss))


def kernel(x_prompt, x_sample, c_prompt, c_sample, state_win_k, state_win_v, cache_k, cache_v,
           page_table, ada_w, ada_b, norm1_w, norm2_w, mlp_w1, mlp_w2, a_wqkv, a_wo, a_sinks,
           b_wqkv, b_wo, b_lambda, b_subln, final_norm_w):
    past_len = page_table.shape[1] * PAGE_SIZE
    return _trunk(x_prompt, x_sample, c_prompt, c_sample, state_win_k, state_win_v, cache_k,
                  cache_v, page_table, ada_w, ada_b, norm1_w, norm2_w, mlp_w1, mlp_w2, a_wqkv,
                  a_wo, a_sinks, b_wqkv, b_wo, b_lambda, b_subln, final_norm_w,
                  past_len=past_len, prompt_rows=min(512, x_prompt.shape[1]),
                  sample_seqs=min(64, x_sample.shape[0]))
```
